```python
import math
import jax, jax.numpy as jnp
from jax import lax
import numpy as np

D_MODEL = 2048
BATCH = 8
SEQ = 4096
DEPTH = 1
DEC_BATCH = 2
DEC_SEQ = 4096
PAST_LEN = 128

MLA_HEADS = 8
MLA_NOPE = 128
MLA_ROPE = 64
MLA_V = 128
Q_LORA = 512
KV_LORA = 256
GQA_HEADS = 8
GQA_KV_HEADS = 2
GQA_HEAD_DIM = 128
MEM_HEADS = 4
MEM_HEAD_DIM = 256
MEM_TOKENS = 256
BRANCH_W = 1024
N_BRANCH = 3
GRID_W = 64
BLOCK_Q = 128
ROPE_BASE = 10000.0
EPS = 1e-6

kernel_name = "hybrid_mla_gqa_mem_gated_encoder"


def _in_sizes():
    return [Q_LORA, KV_LORA, MLA_ROPE,
            GQA_HEADS * GQA_HEAD_DIM, GQA_KV_HEADS * GQA_HEAD_DIM, GQA_KV_HEADS * GQA_HEAD_DIM,
            MEM_HEADS * MEM_HEAD_DIM,
            N_BRANCH * BRANCH_W,
            N_BRANCH * D_MODEL]


def rmsnorm(x, g):
    xf = x.astype(jnp.float32)
    xf = xf * lax.rsqrt(jnp.mean(xf * xf, axis=-1, keepdims=True) + EPS)
    return xf.astype(x.dtype) * g


def grid_positions(s):
    rows = s // GRID_W
    row = jnp.repeat(jnp.arange(rows, dtype=jnp.float32), GRID_W)
    col = jnp.tile(jnp.arange(GRID_W, dtype=jnp.float32), rows)
    return row, col


def rope_1d(x, pos):
    half = x.shape[-1] // 2
    freqs = ROPE_BASE ** (-jnp.arange(half, dtype=jnp.float32) / half)
    ang = pos[:, None] * freqs[None, :]
    cos = jnp.cos(ang)[:, None, :].astype(x.dtype)
    sin = jnp.sin(ang)[:, None, :].astype(x.dtype)
    x1, x2 = x[..., :half], x[..., half:]
    return jnp.concatenate([x1 * cos - x2 * sin, x2 * cos + x1 * sin], axis=-1)


def axial_rope(x, row, col):
    d = x.shape[-1] // 2
    return jnp.concatenate([rope_1d(x[..., :d], row), rope_1d(x[..., d:], col)], axis=-1)


def block_attention(q, k, v, scale):
    b, s, g, r, dk = q.shape
    nb = s // BLOCK_Q
    qb = q.reshape(b, nb, BLOCK_Q, g, r, dk).transpose(1, 0, 2, 3, 4, 5)

    def one(qblk):
        sc = jnp.einsum('bqgrd,bkgd->bgrqk', qblk, k).astype(jnp.float32) * scale
        p = jax.nn.softmax(sc, axis=-1).astype(v.dtype)
        return jnp.einsum('bgrqk,bkgv->bqgrv', p, v)

    out = lax.map(one, qb)
    dv = v.shape[-1]
    return out.transpose(1, 0, 2, 3, 4, 5).reshape(b, s, g * r, dv)


def layer(x, mem, row, col, norm_g, w_in, mla_q_norm_g, w_q_b, mla_kv_norm_g, w_kv_b,
          gqa_q_norm_g, gqa_k_norm_g, mem_norm_g, w_mem_kv, w_branch, w_out):
    b, s, _ = x.shape
    h = rmsnorm(x, norm_g)
    proj = h @ w_in
    idx = np.cumsum(_in_sizes())[:-1].tolist()
    cq, ckv, kpe, qg, kg, vg, qm, z, gl = jnp.split(proj, idx, axis=-1)

    cq = rmsnorm(cq, mla_q_norm_g)
    q = (cq @ w_q_b).reshape(b, s, MLA_HEADS, MLA_NOPE + MLA_ROPE)
    q_nope, q_pe = q[..., :MLA_NOPE], q[..., MLA_NOPE:]
    q_pe = axial_rope(q_pe, row, col)
    ckv = rmsnorm(ckv, mla_kv_norm_g)
    kv = (ckv @ w_kv_b).reshape(b, s, MLA_HEADS, MLA_NOPE + MLA_V)
    k_nope, v_mla = kv[..., :MLA_NOPE], kv[..., MLA_NOPE:]
    k_pe = axial_rope(kpe[:, :, None, :], row, col)
    k_pe = jnp.broadcast_to(k_pe, (b, s, MLA_HEADS, MLA_ROPE))
    q_full = jnp.concatenate([q_nope, q_pe], axis=-1)[:, :, :, None, :]
    k_full = jnp.concatenate([k_nope, k_pe], axis=-1)
    o_mla = block_attention(q_full, k_full, v_mla, 1.0 / math.sqrt(MLA_NOPE + MLA_ROPE))
    o_mla = o_mla.reshape(b, s, BRANCH_W)

    qg = rmsnorm(qg.reshape(b, s, GQA_HEADS, GQA_HEAD_DIM), gqa_q_norm_g)
    kg = rmsnorm(kg.reshape(b, s, GQA_KV_HEADS, GQA_HEAD_DIM), gqa_k_norm_g)
    vg = vg.reshape(b, s, GQA_KV_HEADS, GQA_HEAD_DIM)
    qg = axial_rope(qg, row, col).reshape(b, s, GQA_KV_HEADS, GQA_HEADS // GQA_KV_HEADS, GQA_HEAD_DIM)
    kg = axial_rope(kg, row, col)
    o_gqa = block_attention(qg, kg, vg, 1.0 / math.sqrt(GQA_HEAD_DIM)).reshape(b, s, BRANCH_W)

    m = rmsnorm(mem, mem_norm_g)
    mkv = (m @ w_mem_kv).reshape(b, MEM_TOKENS, 2, MEM_HEADS, MEM_HEAD_DIM)
    km, vm = mkv[:, :, 0], mkv[:, :, 1]
    qm = qm.reshape(b, s, MEM_HEADS, MEM_HEAD_DIM)
    sc = jnp.einsum('bshd,bmhd->bhsm', qm, km).astype(jnp.float32) * (1.0 / math.sqrt(MEM_HEAD_DIM))
    pm = jax.nn.softmax(sc, axis=-1).astype(vm.dtype)
    o_mem = jnp.einsum('bhsm,bmhd->bshd', pm, vm).reshape(b, s, BRANCH_W)

    z = z.reshape(b, s, N_BRANCH, BRANCH_W)
    gl = gl.reshape(b, s, N_BRANCH, D_MODEL)
    merged = None
    for i, o in enumerate((o_mla, o_gqa, o_mem)):
        term = jax.nn.sigmoid(gl[:, :, i]) * ((o * jax.nn.silu(z[:, :, i])) @ w_branch[i])
        merged = term if merged is None else merged + term
    return x + merged @ w_out


def trunk(x, mem, norm_g, w_in, mla_q_norm_g, w_q_b, mla_kv_norm_g, w_kv_b,
          gqa_q_norm_g, gqa_k_norm_g, mem_norm_g, w_mem_kv, w_branch, w_out, final_norm_g):
    row, col = grid_positions(x.shape[1])
    h = x
    for l in range(DEPTH):
        h = layer(h, mem, row, col, norm_g[l], w_in[l], mla_q_norm_g[l], w_q_b[l], mla_kv_norm_g[l],
                  w_kv_b[l], gqa_q_norm_g[l], gqa_k_norm_g[l], mem_norm_g[l], w_mem_kv[l],
                  w_branch[l], w_out[l])
    return rmsnorm(h, final_norm_g)


def setup_inputs(seed: int = 0) -> dict:
    key = jax.random.key(seed)
    ks = jax.random.split(key, 20)
    f32 = jnp.float32
    in_cols = int(sum(_in_sizes()))

    def w(k, shape, fan_in):
        return jax.random.normal(k, shape, f32) * (fan_in ** -0.5)

    def gain(k, shape):
        return 1.0 + 0.02 * jax.random.normal(k, shape, f32)

    return {
        "x_prompt": jax.random.normal(ks[0], (BATCH, SEQ, D_MODEL), f32),
        "x_sample": jax.random.normal(ks[1], (DEC_BATCH, DEC_SEQ, D_MODEL), f32),
        "mem_prompt": jax.random.normal(ks[2], (BATCH, MEM_TOKENS, D_MODEL), f32),
        "mem_sample": jax.random.normal(ks[3], (DEC_BATCH, MEM_TOKENS, D_MODEL), f32),
        "norm_g": gain(ks[4], (DEPTH, D_MODEL)),
        "w_in": w(ks[5], (DEPTH, D_MODEL, in_cols), D_MODEL),
        "mla_q_norm_g": gain(ks[6], (DEPTH, Q_LORA)),
        "w_q_b": w(ks[7], (DEPTH, Q_LORA, MLA_HEADS * (MLA_NOPE + MLA_ROPE)), Q_LORA),
        "mla_kv_norm_g": gain(ks[8], (DEPTH, KV_LORA)),
        "w_kv_b": w(ks[9], (DEPTH, KV_LORA, MLA_HEADS * (MLA_NOPE + MLA_V)), KV_LORA),
        "gqa_q_norm_g": gain(ks[10], (DEPTH, GQA_HEAD_DIM)),
        "gqa_k_norm_g": gain(ks[11], (DEPTH, GQA_HEAD_DIM)),
        "mem_norm_g": gain(ks[12], (DEPTH, D_MODEL)),
        "w_mem_kv": w(ks[13], (DEPTH, D_MODEL, 2 * MEM_HEADS * MEM_HEAD_DIM), D_MODEL),
        "w_branch": w(ks[14], (DEPTH, N_BRANCH, BRANCH_W, D_MODEL), BRANCH_W),
        "w_out": w(ks[15], (DEPTH, D_MODEL, D_MODEL), D_MODEL),
        "final_norm_g": gain(ks[16], (D_MODEL,)),
    }


def reference(x_prompt, x_sample, mem_prompt, mem_sample, norm_g, w_in, mla_q_norm_g, w_q_b,
              mla_kv_norm_g, w_kv_b, gqa_q_norm_g, gqa_k_norm_g, mem_norm_g, w_mem_kv,
              w_branch, w_out, final_norm_g):
    y_prompt = trunk(x_prompt, mem_prompt, norm_g, w_in, mla_q_norm_g, w_q_b, mla_kv_norm_g, w_kv_b,
                     gqa_q_norm_g, gqa_k_norm_g, mem_norm_g, w_mem_kv, w_branch, w_out, final_norm_g)
    y_sample = trunk(x_sample, mem_sample, norm_g, w_in, mla_q_norm_g, w_q_b, mla_kv_norm_g, w_kv_b,
                     gqa_q_norm_g, gqa_k_norm_g, mem_norm_g, w_mem_kv, w_branch, w_out, final_norm_g)
    return (y_prompt, y_sample)
```

```python
import functools
import math

import jax
import jax.numpy as jnp
from jax import lax
from jax.experimental import pallas as pl
from jax.experimental.pallas import tpu as pltpu

F32 = jnp.float32
BF16 = jnp.bfloat16

D_MODEL = 2048
MLA_HEADS = 8
MLA_NOPE = 128
MLA_ROPE = 64
MLA_V = 128
Q_LORA = 512
KV_LORA = 256
GQA_HEADS = 8
GQA_KV_HEADS = 2
GQA_HEAD_DIM = 128
MEM_HEADS = 4
MEM_HEAD_DIM = 256
MEM_TOKENS = 256
BRANCH_W = 1024
N_BRANCH = 3
GRID_W = 64
ROPE_BASE = 10000.0
EPS = 1e-6
LOG2E = math.log2(math.e)

LANES = 128
MLA_QK_PAD = 2 * LANES
V_EXT = 2 * LANES
VMEM_CAP = 60000 * 1024

_C_CQ = 0
_C_CKV = _C_CQ + Q_LORA
_C_KPE = _C_CKV + KV_LORA
_C_QG = _C_KPE + LANES
_C_KG = _C_QG + GQA_HEADS * GQA_HEAD_DIM
_C_VG = _C_KG + GQA_KV_HEADS * GQA_HEAD_DIM
_C_QM = _C_VG + GQA_KV_HEADS * GQA_HEAD_DIM
_C_END = _C_QM + MEM_HEADS * MEM_HEAD_DIM


def _params(semantics, vmem_bytes):
    return pltpu.CompilerParams(dimension_semantics=semantics,
                                vmem_limit_bytes=min(int(vmem_bytes), VMEM_CAP))


def _const_spec(shape):
    nd = len(shape)
    return pl.BlockSpec(shape, lambda *_: (0,) * nd, pipeline_mode=pl.Buffered(1))


def _rms(x):
    return x * lax.rsqrt(jnp.mean(x * x, axis=-1, keepdims=True) + EPS)


def _rope(x, cos, sin_signed, first, half):
    n = x.shape[-1]
    partner = jnp.where(first, pltpu.roll(x, n - half, 1), pltpu.roll(x, half, 1))
    return x * cos + partner * sin_signed


def _norm_kernel(x_ref, g_ref, o_ref):
    o_ref[...] = (_rms(x_ref[...]) * g_ref[...]).astype(o_ref.dtype)


def _norm(x, g, tm):
    rows, d = x.shape
    return pl.pallas_call(
        _norm_kernel,
        grid=(rows // tm,),
        in_specs=[pl.BlockSpec((tm, d), lambda i: (i, 0)), _const_spec((1, d))],
        out_specs=pl.BlockSpec((tm, d), lambda i: (i, 0)),
        out_shape=jax.ShapeDtypeStruct((rows, d), BF16),
        compiler_params=_params(("parallel",), 2 * tm * d * 6 + (8 << 20)),
        name="norm",
    )(x, g)


def _small_kernel(h_ref, w_ref, gq_ref, wqb_ref, gkv_ref, wkvb_ref, gqg_ref, gkg_ref,
                  cosm_ref, sinm_ref, cosg_ref, sing_ref,
                  qmla_ref, kmla_ref, vmla_ref, qg_ref, kg_ref, vg_ref, qm_ref):
    tm = h_ref.shape[0]
    h = h_ref[...]

    def proj(lo, hi):
        return jnp.dot(h, w_ref[:, lo:hi], preferred_element_type=F32)

    lane = lax.broadcasted_iota(jnp.int32, (tm, LANES), 1)
    first_m = (lane % (MLA_ROPE // 2)) < (MLA_ROPE // 4)
    first_g = (lane % (GQA_HEAD_DIM // 2)) < (GQA_HEAD_DIM // 4)
    cosm, sinm = cosm_ref[...], sinm_ref[...]
    cosg, sing = cosg_ref[...], sing_ref[...]
    ones = jnp.ones((tm, LANES), BF16)

    cqn = (_rms(proj(_C_CQ, _C_CKV)) * gq_ref[...]).astype(BF16)
    q = jnp.dot(cqn, wqb_ref[...], preferred_element_type=F32)
    for hd in range(MLA_HEADS):
        c0 = hd * MLA_QK_PAD
        qmla_ref[:, c0:c0 + LANES] = q[:, c0:c0 + LANES].astype(BF16)
        pe = _rope(q[:, c0 + LANES:c0 + 2 * LANES], cosm, sinm, first_m, MLA_ROPE // 4)
        qmla_ref[:, c0 + LANES:c0 + 2 * LANES] = pe.astype(BF16)

    ckvn = (_rms(proj(_C_CKV, _C_KPE)) * gkv_ref[...]).astype(BF16)
    kv = jnp.dot(ckvn, wkvb_ref[...], preferred_element_type=F32)
    kpe = _rope(proj(_C_KPE, _C_QG), cosm, sinm, first_m, MLA_ROPE // 4).astype(BF16)
    nk = MLA_HEADS * MLA_NOPE
    for hd in range(MLA_HEADS):
        c0 = hd * MLA_QK_PAD
        kmla_ref[:, c0:c0 + LANES] = kv[:, hd * MLA_NOPE:(hd + 1) * MLA_NOPE].astype(BF16)
        kmla_ref[:, c0 + LANES:c0 + 2 * LANES] = kpe
        v0 = hd * V_EXT
        vmla_ref[:, v0:v0 + LANES] = kv[:, nk + hd * MLA_V:nk + (hd + 1) * MLA_V].astype(BF16)
        vmla_ref[:, v0 + LANES:v0 + 2 * LANES] = ones

    qg = proj(_C_QG, _C_KG)
    for hd in range(GQA_HEADS):
        sl = slice(hd * GQA_HEAD_DIM, (hd + 1) * GQA_HEAD_DIM)
        xn = _rms(qg[:, sl]) * gqg_ref[...]
        qg_ref[:, sl] = _rope(xn, cosg, sing, first_g, GQA_HEAD_DIM // 4).astype(BF16)
    kg = proj(_C_KG, _C_VG)
    vg = proj(_C_VG, _C_QM)
    for hd in range(GQA_KV_HEADS):
        sl = slice(hd * GQA_HEAD_DIM, (hd + 1) * GQA_HEAD_DIM)
        xn = _rms(kg[:, sl]) * gkg_ref[...]
        kg_ref[:, sl] = _rope(xn, cosg, sing, first_g, GQA_HEAD_DIM // 4).astype(BF16)
        v0 = hd * V_EXT
        vg_ref[:, v0:v0 + LANES] = vg[:, sl].astype(BF16)
        vg_ref[:, v0 + LANES:v0 + 2 * LANES] = ones

    qm_ref[...] = proj(_C_QM, _C_END).astype(BF16)


def _small(h, w_small, gq, wqb, gkv, wkvb, gqg, gkg, tabs, seq, tm):
    rows, d = h.shape
    nblk = seq // tm
    row = lambda i: (i, 0)
    tab = lambda i: (i % nblk, 0)
    widths = (MLA_HEADS * MLA_QK_PAD, MLA_HEADS * MLA_QK_PAD, MLA_HEADS * V_EXT,
              GQA_HEADS * GQA_HEAD_DIM, GQA_KV_HEADS * GQA_HEAD_DIM, GQA_KV_HEADS * V_EXT,
              MEM_HEADS * MEM_HEAD_DIM)
    weights = (w_small, gq, wqb, gkv, wkvb, gqg, gkg)
    wbytes = sum(int(w.size) * w.dtype.itemsize for w in weights)
    vmem = wbytes + 2 * tm * (d * 2 + 4 * LANES * 4 + sum(widths) * 2) + tm * 4096 * 4 * 3 + (8 << 20)
    return pl.pallas_call(
        _small_kernel,
        grid=(rows // tm,),
        in_specs=[pl.BlockSpec((tm, d), row)] + [_const_spec(w.shape) for w in weights]
                 + [pl.BlockSpec((tm, LANES), tab)] * 4,
        out_specs=[pl.BlockSpec((tm, w), row) for w in widths],
        out_shape=[jax.ShapeDtypeStruct((rows, w), BF16) for w in widths],
        compiler_params=_params(("parallel",), vmem),
        name="small_proj",
    )(h, *weights, *tabs)


def _gate_kernel(h_ref, w_ref, o_ref, *, silu):
    a = jnp.dot(h_ref[...], w_ref[...], preferred_element_type=F32)
    s = 1.0 / (1.0 + jnp.exp(-a))
    o_ref[...] = (a * s if silu else s).astype(o_ref.dtype)


def _gate(h, w, silu, tm, tn, name):
    rows, d = h.shape
    n = w.shape[1]
    vmem = 2 * (tm * d * 2 + d * tn * 2 + tm * tn * 2) + 3 * tm * tn * 4 + (8 << 20)
    return pl.pallas_call(
        functools.partial(_gate_kernel, silu=silu),
        grid=(n // tn, rows // tm),
        in_specs=[pl.BlockSpec((tm, d), lambda j, i: (i, 0)),
                  pl.BlockSpec((d, tn), lambda j, i: (0, j))],
        out_specs=pl.BlockSpec((tm, tn), lambda j, i: (i, j)),
        out_shape=jax.ShapeDtypeStruct((rows, n), BF16),
        compiler_params=_params(("parallel", "parallel"), vmem),
        name=name,
    )(h, w)


def _attn_kernel(q_ref, k_ref, v_ref, sz_ref, o_ref, *, c):
    s = lax.dot_general(q_ref[...], k_ref[...], (((1,), (1,)), ((), ())),
                        preferred_element_type=F32)
    m = jnp.max(s, axis=-1, keepdims=True)
    p = jnp.exp2((s - m) * c).astype(BF16)
    oe = jnp.dot(p, v_ref[...], preferred_element_type=F32)
    o = oe[:, :LANES] / oe[:, LANES:]
    o_ref[...] = (o * sz_ref[...].astype(F32)).astype(o_ref.dtype)


def _attn(q, k, v, sz, sz_block0, batch, seq, heads, kv_heads, dk, scale, bq, name):
    rows = batch * seq
    nq = seq // bq
    rep = heads // kv_heads
    vmem = 2 * (bq * dk * 2 + seq * dk * 2 + seq * V_EXT * 2 + 2 * bq * LANES * 2) \
        + bq * seq * (4 + 4 + 2) + (8 << 20)
    return pl.pallas_call(
        functools.partial(_attn_kernel, c=scale * LOG2E),
        grid=(batch, heads, nq),
        in_specs=[pl.BlockSpec((bq, dk), lambda b, h, i: (b * nq + i, h)),
                  pl.BlockSpec((seq, dk), lambda b, h, i: (b, h // rep)),
                  pl.BlockSpec((seq, V_EXT), lambda b, h, i: (b, h // rep)),
                  pl.BlockSpec((bq, LANES), lambda b, h, i: (b * nq + i, sz_block0 + h))],
        out_specs=pl.BlockSpec((bq, LANES), lambda b, h, i: (b * nq + i, h)),
        out_shape=jax.ShapeDtypeStruct((rows, heads * LANES), BF16),
        compiler_params=_params(("parallel", "parallel", "parallel"), vmem),
        name=name,
    )(q, k, v, sz)


def _memkv_kernel(m_ref, g_ref, w_ref, k_ref, v_ref):
    mn = (_rms(m_ref[...]) * g_ref[...]).astype(BF16)
    kv = jnp.dot(mn, w_ref[...], preferred_element_type=F32)
    half = kv.shape[1] // 2
    k_ref[...] = kv[:, :half].astype(BF16)
    v_ref[...] = kv[:, half:].astype(BF16)


def _memkv(mem, g, w):
    rows, d = mem.shape
    n = w.shape[1] // 2
    blk = pl.BlockSpec((MEM_TOKENS, n), lambda b: (b, 0))
    vmem = d * 2 * n * 2 + 2 * MEM_TOKENS * (d * 4 + 2 * n * 2) + MEM_TOKENS * 2 * n * 8 + (8 << 20)
    return pl.pallas_call(
        _memkv_kernel,
        grid=(rows // MEM_TOKENS,),
        in_specs=[pl.BlockSpec((MEM_TOKENS, d), lambda b: (b, 0)),
                  _const_spec(g.shape), _const_spec(w.shape)],
        out_specs=[blk, blk],
        out_shape=[jax.ShapeDtypeStruct((rows, n), BF16)] * 2,
        compiler_params=_params(("parallel",), vmem),
        name="mem_kv",
    )(mem, g, w)


def _memattn_kernel(q_ref, k_ref, v_ref, sz_ref, o_ref):
    c = LOG2E / math.sqrt(MEM_HEAD_DIM)
    for hd in range(MEM_HEADS):
        sl = slice(hd * MEM_HEAD_DIM, (hd + 1) * MEM_HEAD_DIM)
        s = lax.dot_general(q_ref[:, sl], k_ref[:, sl], (((1,), (1,)), ((), ())),
                            preferred_element_type=F32)
        m = jnp.max(s, axis=-1, keepdims=True)
        e = jnp.exp2((s - m) * c)
        l = jnp.sum(e, axis=-1, keepdims=True)
        o = jnp.dot(e.astype(BF16), v_ref[:, sl], preferred_element_type=F32) / l
        o_ref[:, sl] = (o * sz_ref[:, sl].astype(F32)).astype(o_ref.dtype)


def _memattn(qm, km, vm, sz, batch, seq, tm):
    rows, n = qm.shape
    nq = seq // tm
    tile = lambda b, i: (b * nq + i, 0)
    vmem = 2 * (3 * tm * n * 2 + 2 * MEM_TOKENS * n * 2) + 6 * tm * n * 4 + (8 << 20)
    return pl.pallas_call(
        _memattn_kernel,
        grid=(batch, nq),
        in_specs=[pl.BlockSpec((tm, n), tile),
                  pl.BlockSpec((MEM_TOKENS, n), lambda b, i: (b, 0)),
                  pl.BlockSpec((MEM_TOKENS, n), lambda b, i: (b, 0)),
                  pl.BlockSpec((tm, n), lambda b, i: (b * nq + i, N_BRANCH - 1))],
        out_specs=pl.BlockSpec((tm, n), tile),
        out_shape=jax.ShapeDtypeStruct((rows, n), BF16),
        compiler_params=_params(("parallel", "parallel"), vmem),
        name="mem_attn",
    )(qm, km, vm, sz)


def _merge_kernel(oa_ref, ob_ref, oc_ref, sg_ref, wb_ref, o_ref):
    d = o_ref.shape[1]
    acc = None
    for i, r in enumerate((oa_ref, ob_ref, oc_ref)):
        t = jnp.dot(r[...], wb_ref[i], preferred_element_type=F32)
        term = sg_ref[:, i * d:(i + 1) * d].astype(F32) * t
        acc = term if acc is None else acc + term
    o_ref[...] = acc.astype(o_ref.dtype)


def _merge(oa, ob, oc, sg, wb, tm):
    rows, n = oa.shape
    d = wb.shape[2]
    row = lambda i: (i, 0)
    vmem = int(wb.size) * 2 + 2 * tm * (3 * n * 2 + N_BRANCH * d * 2 + d * 2) + 3 * tm * d * 4 + (8 << 20)
    return pl.pallas_call(
        _merge_kernel,
        grid=(rows // tm,),
        in_specs=[pl.BlockSpec((tm, n), row)] * 3
                 + [pl.BlockSpec((tm, N_BRANCH * d), row), _const_spec(wb.shape)],
        out_specs=pl.BlockSpec((tm, d), row),
        out_shape=jax.ShapeDtypeStruct((rows, d), BF16),
        compiler_params=_params(("parallel",), vmem),
        name="merge",
    )(oa, ob, oc, sg, wb)


def _out_kernel(m_ref, x_ref, w_ref, g_ref, o_ref):
    y = x_ref[...] + jnp.dot(m_ref[...], w_ref[...], preferred_element_type=F32)
    o_ref[...] = (_rms(y) * g_ref[...]).astype(o_ref.dtype)


def _outproj(merged, x, w, g, tm):
    rows, d = x.shape
    row = lambda i: (i, 0)
    vmem = d * d * 2 + 2 * tm * d * (2 + 4 + 4) + 3 * tm * d * 4 + (8 << 20)
    return pl.pallas_call(
        _out_kernel,
        grid=(rows // tm,),
        in_specs=[pl.BlockSpec((tm, d), row), pl.BlockSpec((tm, d), row),
                  _const_spec(w.shape), _const_spec(g.shape)],
        out_specs=pl.BlockSpec((tm, d), row),
        out_shape=jax.ShapeDtypeStruct((rows, d), x.dtype),
        compiler_params=_params(("parallel",), vmem),
        name="out_proj",
    )(merged, x, w, g)


def _rope_tables(seq):
    rows = seq // GRID_W
    row = jnp.repeat(jnp.arange(rows, dtype=F32), GRID_W)
    col = jnp.tile(jnp.arange(GRID_W, dtype=F32), rows)

    def table(width):
        half = width // 4
        freqs = ROPE_BASE ** (-jnp.arange(half, dtype=F32) / half)
        parts_c, parts_s = [], []
        for pos in (row, col):
            ang = pos[:, None] * freqs[None, :]
            c, s = jnp.cos(ang), jnp.sin(ang)
            parts_c += [c, c]
            parts_s += [-s, s]
        c = jnp.concatenate(parts_c, axis=-1)
        s = jnp.concatenate(parts_s, axis=-1)
        pad = LANES - width
        if pad:
            c = jnp.pad(c, ((0, 0), (0, pad)))
            s = jnp.pad(s, ((0, 0), (0, pad)))
        return c, s

    cm, sm = table(MLA_ROPE)
    cg, sg = table(GQA_HEAD_DIM)
    return cm, sm, cg, sg


def _prep_layer(w_in, w_q_b, w_kv_b, w_mem_kv, w_branch, w_out):
    d = w_in.shape[0]
    n_small = Q_LORA + KV_LORA + MLA_ROPE
    w_small = jnp.concatenate(
        [w_in[:, :n_small], jnp.zeros((d, LANES - MLA_ROPE), w_in.dtype),
         w_in[:, n_small:n_small + (_C_END - _C_QG)]], axis=1).astype(BF16)
    z0 = n_small + (_C_END - _C_QG)
    w_z = w_in[:, z0:z0 + N_BRANCH * BRANCH_W].astype(BF16)
    w_gl = w_in[:, z0 + N_BRANCH * BRANCH_W:].astype(BF16)
    wqb = w_q_b.reshape(Q_LORA, MLA_HEADS, MLA_NOPE + MLA_ROPE)
    wqb = jnp.pad(wqb, ((0, 0), (0, 0), (0, MLA_QK_PAD - MLA_NOPE - MLA_ROPE)))
    wqb = wqb.reshape(Q_LORA, MLA_HEADS * MLA_QK_PAD).astype(BF16)
    wkv = w_kv_b.reshape(KV_LORA, MLA_HEADS, MLA_NOPE + MLA_V)
    wkvb = jnp.concatenate([wkv[:, :, :MLA_NOPE].reshape(KV_LORA, -1),
                            wkv[:, :, MLA_NOPE:].reshape(KV_LORA, -1)], axis=1).astype(BF16)
    return w_small, w_z, w_gl, wqb, wkvb, w_mem_kv.astype(BF16), w_branch.astype(BF16), w_out.astype(BF16)


def _tile(n, pref):
    t = min(n, pref)
    assert n % t == 0, (n, t)
    return t


def _layer(x, mem, tabs, batch, seq, norm_g, mla_q_norm_g, mla_kv_norm_g, gqa_q_norm_g, gqa_k_norm_g,
           mem_norm_g, final_g, prepped):
    w_small, w_z, w_gl, wqb, wkvb, wmem, wbr, wout = prepped
    rows = batch * seq
    r2 = lambda g: g.reshape(1, -1)
    tm = _tile(seq, 512)

    h = _norm(x, r2(norm_g), tm)
    qmla, kmla, vmla, qg, kg, vg, qm = _small(
        h, w_small, r2(mla_q_norm_g), wqb, r2(mla_kv_norm_g), wkvb, r2(gqa_q_norm_g), r2(gqa_k_norm_g),
        tabs, seq, tm)
    tg = _tile(rows, 1024)
    sz = _gate(h, w_z, True, tg, 1536, "gate_z")
    sg = _gate(h, w_gl, False, tg, 1536, "gate_g")

    bq = _tile(seq, 512)
    o_mla = _attn(qmla, kmla, vmla, sz, 0, batch, seq, MLA_HEADS, MLA_HEADS, MLA_QK_PAD,
                  1.0 / math.sqrt(MLA_NOPE + MLA_ROPE), bq, "attn_mla")
    o_gqa = _attn(qg, kg, vg, sz, BRANCH_W // LANES, batch, seq, GQA_HEADS, GQA_KV_HEADS, GQA_HEAD_DIM,
                  1.0 / math.sqrt(GQA_HEAD_DIM), bq, "attn_gqa")
    km, vm = _memkv(mem, r2(mem_norm_g), wmem)
    o_mem = _memattn(qm, km, vm, sz, batch, seq, tm)

    merged = _merge(o_mla, o_gqa, o_mem, sg, wbr, tm)
    return _outproj(merged, x, wout, r2(final_g), tm)


def _trunk(x, mem, norm_g, w_in, mla_q_norm_g, w_q_b, mla_kv_norm_g, w_kv_b, gqa_q_norm_g, gqa_k_norm_g,
           mem_norm_g, w_mem_kv, w_branch, w_out, final_norm_g, prepped):
    batch, seq, d = x.shape
    depth = w_in.shape[0]
    tabs = _rope_tables(seq)
    hcur = x.reshape(batch * seq, d)
    mem2 = mem.reshape(batch * mem.shape[1], d)
    for l in range(depth):
        assert depth == 1, "multi-layer stacks need an un-normed residual output"
        hcur = _layer(hcur, mem2, tabs, batch, seq, norm_g[l], mla_q_norm_g[l], mla_kv_norm_g[l],
                      gqa_q_norm_g[l], gqa_k_norm_g[l], mem_norm_g[l], final_norm_g, prepped[l])
    return hcur.reshape(batch, seq, d)


def kernel(x_prompt, x_sample, mem_prompt, mem_sample, norm_g, w_in, mla_q_norm_g, w_q_b, mla_kv_norm_g,
           w_kv_b, gqa_q_norm_g, gqa_k_norm_g, mem_norm_g, w_mem_kv, w_branch, w_out, final_norm_g):
    prepped = [_prep_layer(w_in[l], w_q_b[l], w_kv_b[l], w_mem_kv[l], w_branch[l], w_out[l])
               for l in range(w_in.shape[0])]
    args = (norm_g, w_in, mla_q_norm_g, w_q_b, mla_kv_norm_g, w_kv_b, gqa_q_norm_g, gqa_k_norm_g,
            mem_norm_g, w_mem_kv, w_branch, w_out, final_norm_g, prepped)
    return (_trunk(x_prompt, mem_prompt, *args), _trunk(x_sample, mem_sample, *args))
```

```python
import functools
import math

import jax
import jax.numpy as jnp
from jax import lax
from jax.experimental import pallas as pl
from jax.experimental.pallas import tpu as pltpu

F32 = jnp.float32
BF16 = jnp.bfloat16

D_MODEL = 2048
MLA_HEADS = 8
MLA_NOPE = 128
MLA_ROPE = 64
MLA_V = 128
Q_LORA = 512
KV_LORA = 256
GQA_HEADS = 8
GQA_KV_HEADS = 2
GQA_HEAD_DIM = 128
MEM_HEADS = 4
MEM_HEAD_DIM = 256
MEM_TOKENS = 256
BRANCH_W = 1024
N_BRANCH = 3
GRID_W = 64
ROPE_BASE = 10000.0
EPS = 1e-6
LOG2E = math.log2(math.e)

LANES = 128
MLA_QK_PAD = 2 * LANES
V_EXT = 2 * LANES
VMEM_CAP = 60000 * 1024

_C_CQ = 0
_C_CKV = _C_CQ + Q_LORA
_C_KPE = _C_CKV + KV_LORA
_C_QG = _C_KPE + LANES
_C_KG = _C_QG + GQA_HEADS * GQA_HEAD_DIM
_C_VG = _C_KG + GQA_KV_HEADS * GQA_HEAD_DIM
_C_QM = _C_VG + GQA_KV_HEADS * GQA_HEAD_DIM
_C_END = _C_QM + MEM_HEADS * MEM_HEAD_DIM


def _params(semantics, vmem_bytes):
    return pltpu.CompilerParams(dimension_semantics=semantics,
                                vmem_limit_bytes=min(int(vmem_bytes), VMEM_CAP))


def _const_spec(shape):
    nd = len(shape)
    return pl.BlockSpec(shape, lambda *_: (0,) * nd, pipeline_mode=pl.Buffered(1))


def _rms(x):
    return x * lax.rsqrt(jnp.mean(x * x, axis=-1, keepdims=True) + EPS)


def _rope(x, cos, sin_signed, first, half):
    n = x.shape[-1]
    partner = jnp.where(first, pltpu.roll(x, n - half, 1), pltpu.roll(x, half, 1))
    return x * cos + partner * sin_signed


def _norm_kernel(x_ref, g_ref, o_ref):
    o_ref[...] = (_rms(x_ref[...]) * g_ref[...]).astype(o_ref.dtype)


def _norm(x, g, tm):
    rows, d = x.shape
    return pl.pallas_call(
        _norm_kernel,
        grid=(rows // tm,),
        in_specs=[pl.BlockSpec((tm, d), lambda i: (i, 0)), _const_spec((1, d))],
        out_specs=pl.BlockSpec((tm, d), lambda i: (i, 0)),
        out_shape=jax.ShapeDtypeStruct((rows, d), BF16),
        compiler_params=_params(("parallel",), 2 * tm * d * 6 + (8 << 20)),
        name="norm",
    )(x, g)


def _small_kernel(h_ref, w_ref, gq_ref, wqb_ref, gkv_ref, wkvb_ref, gqg_ref, gkg_ref,
                  cosm_ref, sinm_ref, cosg_ref, sing_ref,
                  qmla_ref, kmla_ref, vmla_ref, qg_ref, kg_ref, vg_ref, qm_ref):
    tm = h_ref.shape[0]
    h = h_ref[...]

    def proj(lo, hi):
        return jnp.dot(h, w_ref[:, lo:hi], preferred_element_type=F32)

    lane = lax.broadcasted_iota(jnp.int32, (tm, LANES), 1)
    first_m = (lane % (MLA_ROPE // 2)) < (MLA_ROPE // 4)
    first_g = (lane % (GQA_HEAD_DIM // 2)) < (GQA_HEAD_DIM // 4)
    cosm, sinm = cosm_ref[...], sinm_ref[...]
    cosg, sing = cosg_ref[...], sing_ref[...]
    ones = jnp.ones((tm, LANES), BF16)

    cqn = (_rms(proj(_C_CQ, _C_CKV)) * gq_ref[...]).astype(BF16)
    q = jnp.dot(cqn, wqb_ref[...], preferred_element_type=F32)
    for hd in range(MLA_HEADS):
        c0 = hd * MLA_QK_PAD
        qmla_ref[:, c0:c0 + LANES] = q[:, c0:c0 + LANES].astype(BF16)
        pe = _rope(q[:, c0 + LANES:c0 + 2 * LANES], cosm, sinm, first_m, MLA_ROPE // 4)
        qmla_ref[:, c0 + LANES:c0 + 2 * LANES] = pe.astype(BF16)

    ckvn = (_rms(proj(_C_CKV, _C_KPE)) * gkv_ref[...]).astype(BF16)
    kv = jnp.dot(ckvn, wkvb_ref[...], preferred_element_type=F32)
    kpe = _rope(proj(_C_KPE, _C_QG), cosm, sinm, first_m, MLA_ROPE // 4).astype(BF16)
    nk = MLA_HEADS * MLA_NOPE
    for hd in range(MLA_HEADS):
        c0 = hd * MLA_QK_PAD
        kmla_ref[:, c0:c0 + LANES] = kv[:, hd * MLA_NOPE:(hd + 1) * MLA_NOPE].astype(BF16)
        kmla_ref[:, c0 + LANES:c0 + 2 * LANES] = kpe
        v0 = hd * V_EXT
        vmla_ref[:, v0:v0 + LANES] = kv[:, nk + hd * MLA_V:nk + (hd + 1) * MLA_V].astype(BF16)
        vmla_ref[:, v0 + LANES:v0 + 2 * LANES] = ones

    qg = proj(_C_QG, _C_KG)
    for hd in range(GQA_HEADS):
        sl = slice(hd * GQA_HEAD_DIM, (hd + 1) * GQA_HEAD_DIM)
        xn = _rms(qg[:, sl]) * gqg_ref[...]
        qg_ref[:, sl] = _rope(xn, cosg, sing, first_g, GQA_HEAD_DIM // 4).astype(BF16)
    kg = proj(_C_KG, _C_VG)
    vg = proj(_C_VG, _C_QM)
    for hd in range(GQA_KV_HEADS):
        sl = slice(hd * GQA_HEAD_DIM, (hd + 1) * GQA_HEAD_DIM)
        xn = _rms(kg[:, sl]) * gkg_ref[...]
        kg_ref[:, sl] = _rope(xn, cosg, sing, first_g, GQA_HEAD_DIM // 4).astype(BF16)
        v0 = hd * V_EXT
        vg_ref[:, v0:v0 + LANES] = vg[:, sl].astype(BF16)
        vg_ref[:, v0 + LANES:v0 + 2 * LANES] = ones

    qm_ref[...] = proj(_C_QM, _C_END).astype(BF16)


def _small(h, w_small, gq, wqb, gkv, wkvb, gqg, gkg, tabs, seq, tm):
    rows, d = h.shape
    nblk = seq // tm
    row = lambda i: (i, 0)
    tab = lambda i: (i % nblk, 0)
    widths = (MLA_HEADS * MLA_QK_PAD, MLA_HEADS * MLA_QK_PAD, MLA_HEADS * V_EXT,
              GQA_HEADS * GQA_HEAD_DIM, GQA_KV_HEADS * GQA_HEAD_DIM, GQA_KV_HEADS * V_EXT,
              MEM_HEADS * MEM_HEAD_DIM)
    weights = (w_small, gq, wqb, gkv, wkvb, gqg, gkg)
    wbytes = sum(int(w.size) * w.dtype.itemsize for w in weights)
    vmem = wbytes + 2 * tm * (d * 2 + 4 * LANES * 4 + sum(widths) * 2) + tm * 4096 * 4 * 3 + (8 << 20)
    return pl.pallas_call(
        _small_kernel,
        grid=(rows // tm,),
        in_specs=[pl.BlockSpec((tm, d), row)] + [_const_spec(w.shape) for w in weights]
                 + [pl.BlockSpec((tm, LANES), tab)] * 4,
        out_specs=[pl.BlockSpec((tm, w), row) for w in widths],
        out_shape=[jax.ShapeDtypeStruct((rows, w), BF16) for w in widths],
        compiler_params=_params(("parallel",), vmem),
        name="small_proj",
    )(h, *weights, *tabs)


def _gate_kernel(h_ref, w_ref, o_ref, *, silu):
    a = jnp.dot(h_ref[...], w_ref[...], preferred_element_type=F32)
    s = 1.0 / (1.0 + jnp.exp(-a))
    o_ref[...] = (a * s if silu else s).astype(o_ref.dtype)


def _gate(h, w, silu, tm, tn, name):
    rows, d = h.shape
    n = w.shape[1]
    vmem = 2 * (tm * d * 2 + d * tn * 2 + tm * tn * 2) + 3 * tm * tn * 4 + (8 << 20)
    return pl.pallas_call(
        functools.partial(_gate_kernel, silu=silu),
        grid=(n // tn, rows // tm),
        in_specs=[pl.BlockSpec((tm, d), lambda j, i: (i, 0)),
                  pl.BlockSpec((d, tn), lambda j, i: (0, j))],
        out_specs=pl.BlockSpec((tm, tn), lambda j, i: (i, j)),
        out_shape=jax.ShapeDtypeStruct((rows, n), BF16),
        compiler_params=_params(("parallel", "parallel"), vmem),
        name=name,
    )(h, w)


def _attn_kernel(q_ref, k_ref, v_ref, sz_ref, o_ref, s0, s1, m0, m1, *, c, bq):
    nq = q_ref.shape[0] // bq
    assert nq % 2 == 0 and nq >= 2

    def rows(t):
        return pl.ds(pl.multiple_of(t * bq, bq), bq)

    def scores(t, s_scr, m_scr):
        s = lax.dot_general(q_ref[rows(t), :], k_ref[...], (((1,), (1,)), ((), ())),
                            preferred_element_type=F32)
        s_scr[...] = s
        m_scr[...] = jnp.max(s, axis=-1, keepdims=True)

    def finish(t, s_scr, m_scr):
        p = jnp.exp2((s_scr[...] - m_scr[...]) * c).astype(BF16)
        oe = jnp.dot(p, v_ref[...], preferred_element_type=F32)
        o = oe[:, :LANES] / oe[:, LANES:]
        o_ref[rows(t), :] = (o * sz_ref[rows(t), :].astype(F32)).astype(o_ref.dtype)

    scores(0, s0, m0)

    def pair(j, carry):
        t = 2 * j
        scores(t + 1, s1, m1)
        finish(t, s0, m0)
        scores(t + 2, s0, m0)
        finish(t + 1, s1, m1)
        return carry

    lax.fori_loop(0, nq // 2 - 1, pair, 0)
    scores(nq - 1, s1, m1)
    finish(nq - 2, s0, m0)
    finish(nq - 1, s1, m1)


def _attn(q, k, v, sz, sz_block0, batch, seq, heads, kv_heads, dk, scale, bq, name):
    rows = batch * seq
    rep = heads // kv_heads
    vmem = 2 * (2 * seq * dk * 2 + seq * V_EXT * 2 + 2 * seq * LANES * 2) \
        + bq * seq * (2 * 4 + 4 + 2) + (8 << 20)
    return pl.pallas_call(
        functools.partial(_attn_kernel, c=scale * LOG2E, bq=bq),
        grid=(batch, heads),
        in_specs=[pl.BlockSpec((seq, dk), lambda b, h: (b, h)),
                  pl.BlockSpec((seq, dk), lambda b, h: (b, h // rep)),
                  pl.BlockSpec((seq, V_EXT), lambda b, h: (b, h // rep)),
                  pl.BlockSpec((seq, LANES), lambda b, h: (b, sz_block0 + h))],
        out_specs=pl.BlockSpec((seq, LANES), lambda b, h: (b, h)),
        out_shape=jax.ShapeDtypeStruct((rows, heads * LANES), BF16),
        scratch_shapes=[pltpu.VMEM((bq, seq), F32), pltpu.VMEM((bq, seq), F32),
                        pltpu.VMEM((bq, 1), F32), pltpu.VMEM((bq, 1), F32)],
        compiler_params=_params(("parallel", "parallel"), vmem),
        name=name,
    )(q, k, v, sz)


def _memkv_kernel(m_ref, g_ref, w_ref, k_ref, v_ref):
    mn = (_rms(m_ref[...]) * g_ref[...]).astype(BF16)
    kv = jnp.dot(mn, w_ref[...], preferred_element_type=F32)
    half = kv.shape[1] // 2
    k_ref[...] = kv[:, :half].astype(BF16)
    v_ref[...] = kv[:, half:].astype(BF16)


def _memkv(mem, g, w):
    rows, d = mem.shape
    n = w.shape[1] // 2
    blk = pl.BlockSpec((MEM_TOKENS, n), lambda b: (b, 0))
    vmem = d * 2 * n * 2 + 2 * MEM_TOKENS * (d * 4 + 2 * n * 2) + MEM_TOKENS * 2 * n * 8 + (8 << 20)
    return pl.pallas_call(
        _memkv_kernel,
        grid=(rows // MEM_TOKENS,),
        in_specs=[pl.BlockSpec((MEM_TOKENS, d), lambda b: (b, 0)),
                  _const_spec(g.shape), _const_spec(w.shape)],
        out_specs=[blk, blk],
        out_shape=[jax.ShapeDtypeStruct((rows, n), BF16)] * 2,
        compiler_params=_params(("parallel",), vmem),
        name="mem_kv",
    )(mem, g, w)


def _memattn_kernel(q_ref, k_ref, v_ref, sz_ref, o_ref):
    c = LOG2E / math.sqrt(MEM_HEAD_DIM)
    for hd in range(MEM_HEADS):
        sl = slice(hd * MEM_HEAD_DIM, (hd + 1) * MEM_HEAD_DIM)
        s = lax.dot_general(q_ref[:, sl], k_ref[:, sl], (((1,), (1,)), ((), ())),
                            preferred_element_type=F32)
        m = jnp.max(s, axis=-1, keepdims=True)
        e = jnp.exp2((s - m) * c)
        l = jnp.sum(e, axis=-1, keepdims=True)
        o = jnp.dot(e.astype(BF16), v_ref[:, sl], preferred_element_type=F32) / l
        o_ref[:, sl] = (o * sz_ref[:, sl].astype(F32)).astype(o_ref.dtype)


def _memattn(qm, km, vm, sz, batch, seq, tm):
    rows, n = qm.shape
    nq = seq // tm
    tile = lambda b, i: (b * nq + i, 0)
    vmem = 2 * (3 * tm * n * 2 + 2 * MEM_TOKENS * n * 2) + 6 * tm * n * 4 + (8 << 20)
    return pl.pallas_call(
        _memattn_kernel,
        grid=(batch, nq),
        in_specs=[pl.BlockSpec((tm, n), tile),
                  pl.BlockSpec((MEM_TOKENS, n), lambda b, i: (b, 0)),
                  pl.BlockSpec((MEM_TOKENS, n), lambda b, i: (b, 0)),
                  pl.BlockSpec((tm, n), lambda b, i: (b * nq + i, N_BRANCH - 1))],
        out_specs=pl.BlockSpec((tm, n), tile),
        out_shape=jax.ShapeDtypeStruct((rows, n), BF16),
        compiler_params=_params(("parallel", "parallel"), vmem),
        name="mem_attn",
    )(qm, km, vm, sz)


def _merge_kernel(oa_ref, ob_ref, oc_ref, sg_ref, wb_ref, o_ref):
    d = o_ref.shape[1]
    acc = None
    for i, r in enumerate((oa_ref, ob_ref, oc_ref)):
        t = jnp.dot(r[...], wb_ref[i], preferred_element_type=F32)
        term = sg_ref[:, i * d:(i + 1) * d].astype(F32) * t
        acc = term if acc is None else acc + term
    o_ref[...] = acc.astype(o_ref.dtype)


def _merge(oa, ob, oc, sg, wb, tm):
    rows, n = oa.shape
    d = wb.shape[2]
    row = lambda i: (i, 0)
    vmem = int(wb.size) * 2 + 2 * tm * (3 * n * 2 + N_BRANCH * d * 2 + d * 2) + 3 * tm * d * 4 + (8 << 20)
    return pl.pallas_call(
        _merge_kernel,
        grid=(rows // tm,),
        in_specs=[pl.BlockSpec((tm, n), row)] * 3
                 + [pl.BlockSpec((tm, N_BRANCH * d), row), _const_spec(wb.shape)],
        out_specs=pl.BlockSpec((tm, d), row),
        out_shape=jax.ShapeDtypeStruct((rows, d), BF16),
        compiler_params=_params(("parallel",), vmem),
        name="merge",
    )(oa, ob, oc, sg, wb)


def _out_kernel(m_ref, x_ref, w_ref, g_ref, o_ref):
    y = x_ref[...] + jnp.dot(m_ref[...], w_ref[...], preferred_element_type=F32)
    o_ref[...] = (_rms(y) * g_ref[...]).astype(o_ref.dtype)


def _outproj(merged, x, w, g, tm):
    rows, d = x.shape
    row = lambda i: (i, 0)
    vmem = d * d * 2 + 2 * tm * d * (2 + 4 + 4) + 3 * tm * d * 4 + (8 << 20)
    return pl.pallas_call(
        _out_kernel,
        grid=(rows // tm,),
        in_specs=[pl.BlockSpec((tm, d), row), pl.BlockSpec((tm, d), row),
                  _const_spec(w.shape), _const_spec(g.shape)],
        out_specs=pl.BlockSpec((tm, d), row),
        out_shape=jax.ShapeDtypeStruct((rows, d), x.dtype),
        compiler_params=_params(("parallel",), vmem),
        name="out_proj",
    )(merged, x, w, g)


def _rope_tables(seq):
    rows = seq // GRID_W
    row = jnp.repeat(jnp.arange(rows, dtype=F32), GRID_W)
    col = jnp.tile(jnp.arange(GRID_W, dtype=F32), rows)

    def table(width):
        half = width // 4
        freqs = ROPE_BASE ** (-jnp.arange(half, dtype=F32) / half)
        parts_c, parts_s = [], []
        for pos in (row, col):
            ang = pos[:, None] * freqs[None, :]
            c, s = jnp.cos(ang), jnp.sin(ang)
            parts_c += [c, c]
            parts_s += [-s, s]
        c = jnp.concatenate(parts_c, axis=-1)
        s = jnp.concatenate(parts_s, axis=-1)
        pad = LANES - width
        if pad:
            c = jnp.pad(c, ((0, 0), (0, pad)))
            s = jnp.pad(s, ((0, 0), (0, pad)))
        return c, s

    cm, sm = table(MLA_ROPE)
    cg, sg = table(GQA_HEAD_DIM)
    return cm, sm, cg, sg


def _prep_layer(w_in, w_q_b, w_kv_b, w_mem_kv, w_branch, w_out):
    d = w_in.shape[0]
    n_small = Q_LORA + KV_LORA + MLA_ROPE
    w_small = jnp.concatenate(
        [w_in[:, :n_small], jnp.zeros((d, LANES - MLA_ROPE), w_in.dtype),
         w_in[:, n_small:n_small + (_C_END - _C_QG)]], axis=1).astype(BF16)
    z0 = n_small + (_C_END - _C_QG)
    w_z = w_in[:, z0:z0 + N_BRANCH * BRANCH_W].astype(BF16)
    w_gl = w_in[:, z0 + N_BRANCH * BRANCH_W:].astype(BF16)
    wqb = w_q_b.reshape(Q_LORA, MLA_HEADS, MLA_NOPE + MLA_ROPE)
    wqb = jnp.pad(wqb, ((0, 0), (0, 0), (0, MLA_QK_PAD - MLA_NOPE - MLA_ROPE)))
    wqb = wqb.reshape(Q_LORA, MLA_HEADS * MLA_QK_PAD).astype(BF16)
    wkv = w_kv_b.reshape(KV_LORA, MLA_HEADS, MLA_NOPE + MLA_V)
    wkvb = jnp.concatenate([wkv[:, :, :MLA_NOPE].reshape(KV_LORA, -1),
                            wkv[:, :, MLA_NOPE:].reshape(KV_LORA, -1)], axis=1).astype(BF16)
    return w_small, w_z, w_gl, wqb, wkvb, w_mem_kv.astype(BF16), w_branch.astype(BF16), w_out.astype(BF16)


def _tile(n, pref):
    t = min(n, pref)
    assert n % t == 0, (n, t)
    return t


def _layer(x, mem, tabs, batch, seq, norm_g, mla_q_norm_g, mla_kv_norm_g, gqa_q_norm_g, gqa_k_norm_g,
           mem_norm_g, final_g, prepped):
    w_small, w_z, w_gl, wqb, wkvb, wmem, wbr, wout = prepped
    rows = batch * seq
    r2 = lambda g: g.reshape(1, -1)
    tm = _tile(seq, 512)

    h = _norm(x, r2(norm_g), tm)
    qmla, kmla, vmla, qg, kg, vg, qm = _small(
        h, w_small, r2(mla_q_norm_g), wqb, r2(mla_kv_norm_g), wkvb, r2(gqa_q_norm_g), r2(gqa_k_norm_g),
        tabs, seq, tm)
    tg = _tile(rows, 1024)
    sz = _gate(h, w_z, True, tg, 1536, "gate_z")
    sg = _gate(h, w_gl, False, tg, 1536, "gate_g")

    bq = _tile(seq // 2, 512)
    o_mla = _attn(qmla, kmla, vmla, sz, 0, batch, seq, MLA_HEADS, MLA_HEADS, MLA_QK_PAD,
                  1.0 / math.sqrt(MLA_NOPE + MLA_ROPE), bq, "attn_mla")
    o_gqa = _attn(qg, kg, vg, sz, BRANCH_W // LANES, batch, seq, GQA_HEADS, GQA_KV_HEADS, GQA_HEAD_DIM,
                  1.0 / math.sqrt(GQA_HEAD_DIM), bq, "attn_gqa")
    km, vm = _memkv(mem, r2(mem_norm_g), wmem)
    o_mem = _memattn(qm, km, vm, sz, batch, seq, tm)

    merged = _merge(o_mla, o_gqa, o_mem, sg, wbr, tm)
    return _outproj(merged, x, wout, r2(final_g), tm)


def _trunk(x, mem, norm_g, w_in, mla_q_norm_g, w_q_b, mla_kv_norm_g, w_kv_b, gqa_q_norm_g, gqa_k_norm_g,
           mem_norm_g, w_mem_kv, w_branch, w_out, final_norm_g, prepped):
    batch, seq, d = x.shape
    depth = w_in.shape[0]
    tabs = _rope_tables(seq)
    hcur = x.reshape(batch * seq, d)
    mem2 = mem.reshape(batch * mem.shape[1], d)
    for l in range(depth):
        assert depth == 1, "multi-layer stacks need an un-normed residual output"
        hcur = _layer(hcur, mem2, tabs, batch, seq, norm_g[l], mla_q_norm_g[l], mla_kv_norm_g[l],
                      gqa_q_norm_g[l], gqa_k_norm_g[l], mem_norm_g[l], final_norm_g, prepped[l])
    return hcur.reshape(batch, seq, d)


def kernel(x_prompt, x_sample, mem_prompt, mem_sample, norm_g, w_in, mla_q_norm_g, w_q_b, mla_kv_norm_g,
           w_kv_b, gqa_q_norm_g, gqa_k_norm_g, mem_norm_g, w_mem_kv, w_branch, w_out, final_norm_g):
    prepped = [_prep_layer(w_in[l], w_q_b[l], w_kv_b[l], w_mem_kv[l], w_branch[l], w_out[l])
               for l in range(w_in.shape[0])]
    args = (norm_g, w_in, mla_q_norm_g, w_q_b, mla_kv_norm_g, w_kv_b, gqa_q_norm_g, gqa_k_norm_g,
            mem_norm_g, w_mem_kv, w_branch, w_out, final_norm_g, prepped)
    return (_trunk(x_prompt, mem_prompt, *args), _trunk(x_sample, mem_sample, *args))
```

```python
import functools
import math

import jax
import jax.numpy as jnp
from jax import lax
from jax.experimental import pallas as pl
from jax.experimental.pallas import tpu as pltpu

F32 = jnp.float32
BF16 = jnp.bfloat16

D_MODEL = 2048
MLA_HEADS = 8
MLA_NOPE = 128
MLA_ROPE = 64
MLA_V = 128
Q_LORA = 512
KV_LORA = 256
GQA_HEADS = 8
GQA_KV_HEADS = 2
GQA_HEAD_DIM = 128
MEM_HEADS = 4
MEM_HEAD_DIM = 256
MEM_TOKENS = 256
BRANCH_W = 1024
N_BRANCH = 3
GRID_W = 64
ROPE_BASE = 10000.0
EPS = 1e-6
LOG2E = math.log2(math.e)

LANES = 128
MLA_QK_PAD = 2 * LANES
VT_ROWS = LANES + 16
VMEM_CAP = 60000 * 1024

_C_CQ = 0
_C_CKV = _C_CQ + Q_LORA
_C_KPE = _C_CKV + KV_LORA
_C_QG = _C_KPE + LANES
_C_KG = _C_QG + GQA_HEADS * GQA_HEAD_DIM
_C_VG = _C_KG + GQA_KV_HEADS * GQA_HEAD_DIM
_C_QM = _C_VG + GQA_KV_HEADS * GQA_HEAD_DIM
_C_END = _C_QM + MEM_HEADS * MEM_HEAD_DIM


def _params(semantics, vmem_bytes):
    return pltpu.CompilerParams(dimension_semantics=semantics,
                                vmem_limit_bytes=min(int(vmem_bytes), VMEM_CAP))


def _const_spec(shape):
    nd = len(shape)
    return pl.BlockSpec(shape, lambda *_: (0,) * nd, pipeline_mode=pl.Buffered(1))


def _rms(x):
    return x * lax.rsqrt(jnp.mean(x * x, axis=-1, keepdims=True) + EPS)


def _rope(x, cos, sin_signed, first, half):
    n = x.shape[-1]
    partner = jnp.where(first, pltpu.roll(x, n - half, 1), pltpu.roll(x, half, 1))
    return x * cos + partner * sin_signed


def _norm_kernel(x_ref, g_ref, o_ref):
    o_ref[...] = (_rms(x_ref[...]) * g_ref[...]).astype(o_ref.dtype)


def _norm(x, g, tm):
    rows, d = x.shape
    return pl.pallas_call(
        _norm_kernel,
        grid=(rows // tm,),
        in_specs=[pl.BlockSpec((tm, d), lambda i: (i, 0)), _const_spec((1, d))],
        out_specs=pl.BlockSpec((tm, d), lambda i: (i, 0)),
        out_shape=jax.ShapeDtypeStruct((rows, d), BF16),
        compiler_params=_params(("parallel",), 2 * tm * d * 6 + (8 << 20)),
        name="norm",
    )(x, g)


def _small_kernel(h_ref, w_ref, wvgt_ref, gq_ref, wqb_ref, gkv_ref, wkb_ref, wvbt_ref, gqg_ref, gkg_ref,
                  cosm_ref, sinm_ref, cosg_ref, sing_ref,
                  qmla_ref, kmla_ref, vmla_ref, qg_ref, kg_ref, vg_ref, qm_ref):
    tm = h_ref.shape[0]
    h = h_ref[...]

    def proj(lo, hi):
        return jnp.dot(h, w_ref[:, lo:hi], preferred_element_type=F32)

    lane = lax.broadcasted_iota(jnp.int32, (tm, LANES), 1)
    first_m = (lane % (MLA_ROPE // 2)) < (MLA_ROPE // 4)
    first_g = (lane % (GQA_HEAD_DIM // 2)) < (GQA_HEAD_DIM // 4)
    cosm, sinm = cosm_ref[...], sinm_ref[...]
    cosg, sing = cosg_ref[...], sing_ref[...]
    ones = jnp.ones((VT_ROWS - LANES, tm), BF16)
    nt = (((1,), (1,)), ((), ()))

    cqn = (_rms(proj(_C_CQ, _C_CKV)) * gq_ref[...]).astype(BF16)
    q = jnp.dot(cqn, wqb_ref[...], preferred_element_type=F32)
    for hd in range(MLA_HEADS):
        c0 = hd * MLA_QK_PAD
        qmla_ref[:, c0:c0 + LANES] = q[:, c0:c0 + LANES].astype(BF16)
        pe = _rope(q[:, c0 + LANES:c0 + 2 * LANES], cosm, sinm, first_m, MLA_ROPE // 4)
        qmla_ref[:, c0 + LANES:c0 + 2 * LANES] = pe.astype(BF16)

    ckvn = (_rms(proj(_C_CKV, _C_KPE)) * gkv_ref[...]).astype(BF16)
    kn = jnp.dot(ckvn, wkb_ref[...], preferred_element_type=F32)
    vt = lax.dot_general(wvbt_ref[...], ckvn, nt, preferred_element_type=F32)
    kpe = _rope(proj(_C_KPE, _C_QG), cosm, sinm, first_m, MLA_ROPE // 4).astype(BF16)
    for hd in range(MLA_HEADS):
        c0 = hd * MLA_QK_PAD
        kmla_ref[:, c0:c0 + LANES] = kn[:, hd * MLA_NOPE:(hd + 1) * MLA_NOPE].astype(BF16)
        kmla_ref[:, c0 + LANES:c0 + 2 * LANES] = kpe
        v0 = hd * VT_ROWS
        vmla_ref[v0:v0 + LANES, :] = vt[hd * MLA_V:(hd + 1) * MLA_V].astype(BF16)
        vmla_ref[v0 + LANES:v0 + VT_ROWS, :] = ones

    qg = proj(_C_QG, _C_KG)
    for hd in range(GQA_HEADS):
        sl = slice(hd * GQA_HEAD_DIM, (hd + 1) * GQA_HEAD_DIM)
        xn = _rms(qg[:, sl]) * gqg_ref[...]
        qg_ref[:, sl] = _rope(xn, cosg, sing, first_g, GQA_HEAD_DIM // 4).astype(BF16)
    kg = proj(_C_KG, _C_VG)
    vgt = lax.dot_general(wvgt_ref[...], h, nt, preferred_element_type=F32)
    for hd in range(GQA_KV_HEADS):
        sl = slice(hd * GQA_HEAD_DIM, (hd + 1) * GQA_HEAD_DIM)
        xn = _rms(kg[:, sl]) * gkg_ref[...]
        kg_ref[:, sl] = _rope(xn, cosg, sing, first_g, GQA_HEAD_DIM // 4).astype(BF16)
        v0 = hd * VT_ROWS
        vg_ref[v0:v0 + LANES, :] = vgt[sl].astype(BF16)
        vg_ref[v0 + LANES:v0 + VT_ROWS, :] = ones

    qm_ref[...] = proj(_C_QM, _C_END).astype(BF16)


def _small(h, w_small, wvgt, gq, wqb, gkv, wkb, wvbt, gqg, gkg, tabs, seq, tm):
    rows, d = h.shape
    nblk = seq // tm
    row = lambda i: (i, 0)
    col = lambda i: (0, i)
    tab = lambda i: (i % nblk, 0)
    outs = ((MLA_HEADS * MLA_QK_PAD, False), (MLA_HEADS * MLA_QK_PAD, False), (MLA_HEADS * VT_ROWS, True),
            (GQA_HEADS * GQA_HEAD_DIM, False), (GQA_KV_HEADS * GQA_HEAD_DIM, False),
            (GQA_KV_HEADS * VT_ROWS, True), (MEM_HEADS * MEM_HEAD_DIM, False))
    weights = (w_small, wvgt, gq, wqb, gkv, wkb, wvbt, gqg, gkg)
    wbytes = sum(int(w.size) * w.dtype.itemsize for w in weights)
    vmem = wbytes + 2 * tm * (d * 2 + 4 * LANES * 4 + sum(w for w, _ in outs) * 2) + tm * 4096 * 4 * 3 + (8 << 20)
    return pl.pallas_call(
        _small_kernel,
        grid=(rows // tm,),
        in_specs=[pl.BlockSpec((tm, d), row)] + [_const_spec(w.shape) for w in weights]
                 + [pl.BlockSpec((tm, LANES), tab)] * 4,
        out_specs=[pl.BlockSpec((w, tm), col) if t else pl.BlockSpec((tm, w), row) for w, t in outs],
        out_shape=[jax.ShapeDtypeStruct((w, rows) if t else (rows, w), BF16) for w, t in outs],
        compiler_params=_params(("parallel",), vmem),
        name="small_proj",
    )(h, *weights, *tabs)


def _gate_kernel(h_ref, w_ref, o_ref, *, silu):
    a = jnp.dot(h_ref[...], w_ref[...], preferred_element_type=F32)
    s = 1.0 / (1.0 + jnp.exp(-a))
    o_ref[...] = (a * s if silu else s).astype(o_ref.dtype)


def _gate(h, w, silu, tm, tn, name):
    rows, d = h.shape
    n = w.shape[1]
    vmem = 2 * (tm * d * 2 + d * tn * 2 + tm * tn * 2) + 3 * tm * tn * 4 + (8 << 20)
    return pl.pallas_call(
        functools.partial(_gate_kernel, silu=silu),
        grid=(n // tn, rows // tm),
        in_specs=[pl.BlockSpec((tm, d), lambda j, i: (i, 0)),
                  pl.BlockSpec((d, tn), lambda j, i: (0, j))],
        out_specs=pl.BlockSpec((tm, tn), lambda j, i: (i, j)),
        out_shape=jax.ShapeDtypeStruct((rows, n), BF16),
        compiler_params=_params(("parallel", "parallel"), vmem),
        name=name,
    )(h, w)


def _attn_kernel(q_ref, k_ref, vt_ref, sz_ref, o_ref, s0, s1, m0, m1, *, c, bq):
    nq = q_ref.shape[0] // bq
    assert nq % 2 == 0 and nq >= 2

    def rows(t):
        return pl.ds(pl.multiple_of(t * bq, bq), bq)

    def scores(t, s_scr, m_scr):
        st = lax.dot_general(k_ref[...], q_ref[rows(t), :], (((1,), (1,)), ((), ())),
                             preferred_element_type=F32)
        s_scr[...] = st
        m_scr[...] = jnp.max(st, axis=0, keepdims=True)

    def finish(t, s_scr, m_scr):
        pt = jnp.exp2((s_scr[...] - m_scr[...]) * c).astype(BF16)
        ot = jnp.dot(vt_ref[...], pt, preferred_element_type=F32)
        o = (ot[:LANES] / ot[LANES:LANES + 1]).T
        o_ref[rows(t), :] = (o * sz_ref[rows(t), :].astype(F32)).astype(o_ref.dtype)

    scores(0, s0, m0)

    def pair(j, carry):
        t = 2 * j
        scores(t + 1, s1, m1)
        finish(t, s0, m0)
        scores(t + 2, s0, m0)
        finish(t + 1, s1, m1)
        return carry

    lax.fori_loop(0, nq // 2 - 1, pair, 0)
    scores(nq - 1, s1, m1)
    finish(nq - 2, s0, m0)
    finish(nq - 1, s1, m1)


def _attn(q, k, vt, sz, sz_block0, batch, seq, heads, kv_heads, dk, scale, bq, name):
    rows = batch * seq
    rep = heads // kv_heads
    vmem = 2 * (2 * seq * dk * 2 + VT_ROWS * seq * 2 + 2 * seq * LANES * 2) \
        + bq * seq * (2 * 4 + 4 + 2) + (8 << 20)
    return pl.pallas_call(
        functools.partial(_attn_kernel, c=scale * LOG2E, bq=bq),
        grid=(batch, heads),
        in_specs=[pl.BlockSpec((seq, dk), lambda b, h: (b, h)),
                  pl.BlockSpec((seq, dk), lambda b, h: (b, h // rep)),
                  pl.BlockSpec((VT_ROWS, seq), lambda b, h: (h // rep, b)),
                  pl.BlockSpec((seq, LANES), lambda b, h: (b, sz_block0 + h))],
        out_specs=pl.BlockSpec((seq, LANES), lambda b, h: (b, h)),
        out_shape=jax.ShapeDtypeStruct((rows, heads * LANES), BF16),
        scratch_shapes=[pltpu.VMEM((seq, bq), F32), pltpu.VMEM((seq, bq), F32),
                        pltpu.VMEM((1, bq), F32), pltpu.VMEM((1, bq), F32)],
        compiler_params=_params(("parallel", "parallel"), vmem),
        name=name,
    )(q, k, vt, sz)


def _memkv_kernel(m_ref, g_ref, w_ref, k_ref, v_ref):
    mn = (_rms(m_ref[...]) * g_ref[...]).astype(BF16)
    kv = jnp.dot(mn, w_ref[...], preferred_element_type=F32)
    half = kv.shape[1] // 2
    k_ref[...] = kv[:, :half].astype(BF16)
    v_ref[...] = kv[:, half:].astype(BF16)


def _memkv(mem, g, w):
    rows, d = mem.shape
    n = w.shape[1] // 2
    blk = pl.BlockSpec((MEM_TOKENS, n), lambda b: (b, 0))
    vmem = d * 2 * n * 2 + 2 * MEM_TOKENS * (d * 4 + 2 * n * 2) + MEM_TOKENS * 2 * n * 8 + (8 << 20)
    return pl.pallas_call(
        _memkv_kernel,
        grid=(rows // MEM_TOKENS,),
        in_specs=[pl.BlockSpec((MEM_TOKENS, d), lambda b: (b, 0)),
                  _const_spec(g.shape), _const_spec(w.shape)],
        out_specs=[blk, blk],
        out_shape=[jax.ShapeDtypeStruct((rows, n), BF16)] * 2,
        compiler_params=_params(("parallel",), vmem),
        name="mem_kv",
    )(mem, g, w)


def _memattn_kernel(q_ref, k_ref, v_ref, sz_ref, o_ref):
    c = LOG2E / math.sqrt(MEM_HEAD_DIM)
    for hd in range(MEM_HEADS):
        sl = slice(hd * MEM_HEAD_DIM, (hd + 1) * MEM_HEAD_DIM)
        s = lax.dot_general(q_ref[:, sl], k_ref[:, sl], (((1,), (1,)), ((), ())),
                            preferred_element_type=F32)
        m = jnp.max(s, axis=-1, keepdims=True)
        e = jnp.exp2((s - m) * c)
        l = jnp.sum(e, axis=-1, keepdims=True)
        o = jnp.dot(e.astype(BF16), v_ref[:, sl], preferred_element_type=F32) / l
        o_ref[:, sl] = (o * sz_ref[:, sl].astype(F32)).astype(o_ref.dtype)


def _memattn(qm, km, vm, sz, batch, seq, tm):
    rows, n = qm.shape
    nq = seq // tm
    tile = lambda b, i: (b * nq + i, 0)
    vmem = 2 * (3 * tm * n * 2 + 2 * MEM_TOKENS * n * 2) + 6 * tm * n * 4 + (8 << 20)
    return pl.pallas_call(
        _memattn_kernel,
        grid=(batch, nq),
        in_specs=[pl.BlockSpec((tm, n), tile),
                  pl.BlockSpec((MEM_TOKENS, n), lambda b, i: (b, 0)),
                  pl.BlockSpec((MEM_TOKENS, n), lambda b, i: (b, 0)),
                  pl.BlockSpec((tm, n), lambda b, i: (b * nq + i, N_BRANCH - 1))],
        out_specs=pl.BlockSpec((tm, n), tile),
        out_shape=jax.ShapeDtypeStruct((rows, n), BF16),
        compiler_params=_params(("parallel", "parallel"), vmem),
        name="mem_attn",
    )(qm, km, vm, sz)


def _merge_kernel(oa_ref, ob_ref, oc_ref, sg_ref, wb_ref, o_ref):
    d = o_ref.shape[1]
    acc = None
    for i, r in enumerate((oa_ref, ob_ref, oc_ref)):
        t = jnp.dot(r[...], wb_ref[i], preferred_element_type=F32)
        term = sg_ref[:, i * d:(i + 1) * d].astype(F32) * t
        acc = term if acc is None else acc + term
    o_ref[...] = acc.astype(o_ref.dtype)


def _merge(oa, ob, oc, sg, wb, tm):
    rows, n = oa.shape
    d = wb.shape[2]
    row = lambda i: (i, 0)
    vmem = int(wb.size) * 2 + 2 * tm * (3 * n * 2 + N_BRANCH * d * 2 + d * 2) + 3 * tm * d * 4 + (8 << 20)
    return pl.pallas_call(
        _merge_kernel,
        grid=(rows // tm,),
        in_specs=[pl.BlockSpec((tm, n), row)] * 3
                 + [pl.BlockSpec((tm, N_BRANCH * d), row), _const_spec(wb.shape)],
        out_specs=pl.BlockSpec((tm, d), row),
        out_shape=jax.ShapeDtypeStruct((rows, d), BF16),
        compiler_params=_params(("parallel",), vmem),
        name="merge",
    )(oa, ob, oc, sg, wb)


def _out_kernel(m_ref, x_ref, w_ref, g_ref, o_ref):
    y = x_ref[...] + jnp.dot(m_ref[...], w_ref[...], preferred_element_type=F32)
    o_ref[...] = (_rms(y) * g_ref[...]).astype(o_ref.dtype)


def _outproj(merged, x, w, g, tm):
    rows, d = x.shape
    row = lambda i: (i, 0)
    vmem = d * d * 2 + 2 * tm * d * (2 + 4 + 4) + 3 * tm * d * 4 + (8 << 20)
    return pl.pallas_call(
        _out_kernel,
        grid=(rows // tm,),
        in_specs=[pl.BlockSpec((tm, d), row), pl.BlockSpec((tm, d), row),
                  _const_spec(w.shape), _const_spec(g.shape)],
        out_specs=pl.BlockSpec((tm, d), row),
        out_shape=jax.ShapeDtypeStruct((rows, d), x.dtype),
        compiler_params=_params(("parallel",), vmem),
        name="out_proj",
    )(merged, x, w, g)


def _rope_tables(seq):
    rows = seq // GRID_W
    row = jnp.repeat(jnp.arange(rows, dtype=F32), GRID_W)
    col = jnp.tile(jnp.arange(GRID_W, dtype=F32), rows)

    def table(width):
        half = width // 4
        freqs = ROPE_BASE ** (-jnp.arange(half, dtype=F32) / half)
        parts_c, parts_s = [], []
        for pos in (row, col):
            ang = pos[:, None] * freqs[None, :]
            c, s = jnp.cos(ang), jnp.sin(ang)
            parts_c += [c, c]
            parts_s += [-s, s]
        c = jnp.concatenate(parts_c, axis=-1)
        s = jnp.concatenate(parts_s, axis=-1)
        pad = LANES - width
        if pad:
            c = jnp.pad(c, ((0, 0), (0, pad)))
            s = jnp.pad(s, ((0, 0), (0, pad)))
        return c, s

    cm, sm = table(MLA_ROPE)
    cg, sg = table(GQA_HEAD_DIM)
    return cm, sm, cg, sg


def _prep_layer(w_in, w_q_b, w_kv_b, w_mem_kv, w_branch, w_out):
    d = w_in.shape[0]
    n_small = Q_LORA + KV_LORA + MLA_ROPE
    w_small = jnp.concatenate(
        [w_in[:, :n_small], jnp.zeros((d, LANES - MLA_ROPE), w_in.dtype),
         w_in[:, n_small:n_small + (_C_END - _C_QG)]], axis=1).astype(BF16)
    vg0 = n_small + (_C_VG - _C_QG)
    wvgt = w_in[:, vg0:vg0 + (_C_QM - _C_VG)].T.astype(BF16)
    z0 = n_small + (_C_END - _C_QG)
    w_z = w_in[:, z0:z0 + N_BRANCH * BRANCH_W].astype(BF16)
    w_gl = w_in[:, z0 + N_BRANCH * BRANCH_W:].astype(BF16)
    wqb = w_q_b.reshape(Q_LORA, MLA_HEADS, MLA_NOPE + MLA_ROPE)
    wqb = jnp.pad(wqb, ((0, 0), (0, 0), (0, MLA_QK_PAD - MLA_NOPE - MLA_ROPE)))
    wqb = wqb.reshape(Q_LORA, MLA_HEADS * MLA_QK_PAD).astype(BF16)
    wkv = w_kv_b.reshape(KV_LORA, MLA_HEADS, MLA_NOPE + MLA_V)
    wkb = wkv[:, :, :MLA_NOPE].reshape(KV_LORA, -1).astype(BF16)
    wvbt = wkv[:, :, MLA_NOPE:].reshape(KV_LORA, -1).T.astype(BF16)
    return (w_small, wvgt, w_z, w_gl, wqb, wkb, wvbt, w_mem_kv.astype(BF16), w_branch.astype(BF16),
            w_out.astype(BF16))


def _tile(n, pref):
    t = min(n, pref)
    assert n % t == 0, (n, t)
    return t


def _layer(x, mem, tabs, batch, seq, norm_g, mla_q_norm_g, mla_kv_norm_g, gqa_q_norm_g, gqa_k_norm_g,
           mem_norm_g, final_g, prepped):
    w_small, wvgt, w_z, w_gl, wqb, wkb, wvbt, wmem, wbr, wout = prepped
    rows = batch * seq
    r2 = lambda g: g.reshape(1, -1)
    tm = _tile(seq, 512)

    h = _norm(x, r2(norm_g), tm)
    qmla, kmla, vmla, qg, kg, vg, qm = _small(
        h, w_small, wvgt, r2(mla_q_norm_g), wqb, r2(mla_kv_norm_g), wkb, wvbt, r2(gqa_q_norm_g), r2(gqa_k_norm_g),
        tabs, seq, tm)
    tg = _tile(rows, 1024)
    sz = _gate(h, w_z, True, tg, 1536, "gate_z")
    sg = _gate(h, w_gl, False, tg, 1536, "gate_g")

    bq = _tile(seq // 2, 512)
    o_mla = _attn(qmla, kmla, vmla, sz, 0, batch, seq, MLA_HEADS, MLA_HEADS, MLA_QK_PAD,
                  1.0 / math.sqrt(MLA_NOPE + MLA_ROPE), bq, "attn_mla")
    o_gqa = _attn(qg, kg, vg, sz, BRANCH_W // LANES, batch, seq, GQA_HEADS, GQA_KV_HEADS, GQA_HEAD_DIM,
                  1.0 / math.sqrt(GQA_HEAD_DIM), bq, "attn_gqa")
    km, vm = _memkv(mem, r2(mem_norm_g), wmem)
    o_mem = _memattn(qm, km, vm, sz, batch, seq, tm)

    merged = _merge(o_mla, o_gqa, o_mem, sg, wbr, tm)
    return _outproj(merged, x, wout, r2(final_g), tm)


def _trunk(x, mem, tabs, gains, prepped):
    batch, seq, d = x.shape
    y = _layer(x.reshape(batch * seq, d), mem.reshape(batch * mem.shape[1], d), tabs, batch, seq, *gains,
               prepped)
    return y.reshape(batch, seq, d)


def kernel(x_prompt, x_sample, mem_prompt, mem_sample, norm_g, w_in, mla_q_norm_g, w_q_b, mla_kv_norm_g,
           w_kv_b, gqa_q_norm_g, gqa_k_norm_g, mem_norm_g, w_mem_kv, w_branch, w_out, final_norm_g):
    assert w_in.shape[0] == 1
    first = lambda a: a.reshape(a.shape[1:])
    prepped = _prep_layer(first(w_in), first(w_q_b), first(w_kv_b), first(w_mem_kv), first(w_branch),
                          first(w_out))
    gains = (first(norm_g), first(mla_q_norm_g), first(mla_kv_norm_g), first(gqa_q_norm_g),
             first(gqa_k_norm_g), first(mem_norm_g), final_norm_g)
    assert x_prompt.shape[1] == x_sample.shape[1]
    tabs = _rope_tables(x_prompt.shape[1])
    return (_trunk(x_prompt, mem_prompt, tabs, gains, prepped),
            _trunk(x_sample, mem_sample, tabs, gains, prepped))
```

```python
import functools
import math

import jax
import jax.numpy as jnp
import numpy as np
from jax import lax
from jax.experimental import pallas as pl
from jax.experimental.pallas import tpu as pltpu

F32 = jnp.float32
BF16 = jnp.bfloat16

D_MODEL = 2048
MLA_HEADS = 8
MLA_NOPE = 128
MLA_ROPE = 64
MLA_V = 128
Q_LORA = 512
KV_LORA = 256
GQA_HEADS = 8
GQA_KV_HEADS = 2
GQA_HEAD_DIM = 128
MEM_HEADS = 4
MEM_HEAD_DIM = 256
MEM_TOKENS = 256
BRANCH_W = 1024
N_BRANCH = 3
GRID_W = 64
ROPE_BASE = 10000.0
EPS = 1e-6
LOG2E = math.log2(math.e)
MLA_SCALE = LOG2E / math.sqrt(MLA_NOPE + MLA_ROPE)
GQA_SCALE = LOG2E / math.sqrt(GQA_HEAD_DIM)

LANES = 128
MLA_QK_PAD = 2 * LANES
VT_ROWS = LANES + 16
VMEM_CAP = 60000 * 1024

_C_CQ = 0
_C_CKV = _C_CQ + Q_LORA
_C_KPE = _C_CKV + KV_LORA
_C_QG = _C_KPE + LANES
_C_KG = _C_QG + GQA_HEADS * GQA_HEAD_DIM
_C_VG = _C_KG + GQA_KV_HEADS * GQA_HEAD_DIM
_C_QM = _C_VG + GQA_KV_HEADS * GQA_HEAD_DIM
_C_END = _C_QM + MEM_HEADS * MEM_HEAD_DIM


def _params(semantics, vmem_bytes):
    return pltpu.CompilerParams(dimension_semantics=semantics,
                                vmem_limit_bytes=min(int(vmem_bytes), VMEM_CAP))


def _const_spec(shape):
    nd = len(shape)
    return pl.BlockSpec(shape, lambda *_: (0,) * nd, pipeline_mode=pl.Buffered(1))


def _rms(x):
    return x * lax.rsqrt(jnp.mean(x * x, axis=-1, keepdims=True) + EPS)


def _rope(x, cos, sin_signed, first, half):
    n = x.shape[-1]
    partner = jnp.where(first, pltpu.roll(x, n - half, 1), pltpu.roll(x, half, 1))
    return x * cos + partner * sin_signed


def _small_kernel(x_ref, gx_ref, w_ref, wvgt_ref, gq_ref, wqb_ref, gkv_ref, wkb_ref, wvbt_ref, gqg_ref, gkg_ref,
                  cosm_ref, sinm_ref, cosg_ref, sing_ref,
                  h_ref, qmla_ref, kmla_ref, vmla_ref, qg_ref, kg_ref, vg_ref, qm_ref):
    tm = x_ref.shape[0]
    h = (_rms(x_ref[...]) * gx_ref[...]).astype(BF16)
    h_ref[...] = h

    def proj(lo, hi):
        return jnp.dot(h, w_ref[:, lo:hi], preferred_element_type=F32)

    lane = lax.broadcasted_iota(jnp.int32, (tm, LANES), 1)
    first_m = (lane % (MLA_ROPE // 2)) < (MLA_ROPE // 4)
    first_g = (lane % (GQA_HEAD_DIM // 2)) < (GQA_HEAD_DIM // 4)
    cosm, sinm = cosm_ref[...], sinm_ref[...]
    cosg, sing = cosg_ref[...], sing_ref[...]
    ones = jnp.ones((VT_ROWS - LANES, tm), BF16)
    nt = (((1,), (1,)), ((), ()))

    cqn = (_rms(proj(_C_CQ, _C_CKV)) * gq_ref[...]).astype(BF16)
    q = jnp.dot(cqn, wqb_ref[...], preferred_element_type=F32)
    for hd in range(MLA_HEADS):
        c0 = hd * MLA_QK_PAD
        qmla_ref[:, c0:c0 + LANES] = (q[:, c0:c0 + LANES] * MLA_SCALE).astype(BF16)
        pe = _rope(q[:, c0 + LANES:c0 + 2 * LANES], cosm, sinm, first_m, MLA_ROPE // 4)
        qmla_ref[:, c0 + LANES:c0 + 2 * LANES] = (pe * MLA_SCALE).astype(BF16)

    ckvn = (_rms(proj(_C_CKV, _C_KPE)) * gkv_ref[...]).astype(BF16)
    kn = jnp.dot(ckvn, wkb_ref[...], preferred_element_type=F32)
    vt = lax.dot_general(wvbt_ref[...], ckvn, nt, preferred_element_type=F32)
    kpe = _rope(proj(_C_KPE, _C_QG), cosm, sinm, first_m, MLA_ROPE // 4).astype(BF16)
    for hd in range(MLA_HEADS):
        c0 = hd * MLA_QK_PAD
        kmla_ref[:, c0:c0 + LANES] = kn[:, hd * MLA_NOPE:(hd + 1) * MLA_NOPE].astype(BF16)
        kmla_ref[:, c0 + LANES:c0 + 2 * LANES] = kpe
        v0 = hd * VT_ROWS
        vmla_ref[v0:v0 + LANES, :] = vt[hd * MLA_V:(hd + 1) * MLA_V].astype(BF16)
        vmla_ref[v0 + LANES:v0 + VT_ROWS, :] = ones

    qg = proj(_C_QG, _C_KG)
    for hd in range(GQA_HEADS):
        sl = slice(hd * GQA_HEAD_DIM, (hd + 1) * GQA_HEAD_DIM)
        xn = _rms(qg[:, sl]) * gqg_ref[...]
        qg_ref[:, sl] = (_rope(xn, cosg, sing, first_g, GQA_HEAD_DIM // 4) * GQA_SCALE).astype(BF16)
    kg = proj(_C_KG, _C_VG)
    vgt = lax.dot_general(wvgt_ref[...], h, nt, preferred_element_type=F32)
    for hd in range(GQA_KV_HEADS):
        sl = slice(hd * GQA_HEAD_DIM, (hd + 1) * GQA_HEAD_DIM)
        xn = _rms(kg[:, sl]) * gkg_ref[...]
        kg_ref[:, sl] = _rope(xn, cosg, sing, first_g, GQA_HEAD_DIM // 4).astype(BF16)
        v0 = hd * VT_ROWS
        vg_ref[v0:v0 + LANES, :] = vgt[sl].astype(BF16)
        vg_ref[v0 + LANES:v0 + VT_ROWS, :] = ones

    qm_ref[...] = proj(_C_QM, _C_END).astype(BF16)


def _small(x, gx, w_small, wvgt, gq, wqb, gkv, wkb, wvbt, gqg, gkg, tabs, seq, tm):
    rows, d = x.shape
    nblk = seq // tm
    row = lambda i: (i, 0)
    col = lambda i: (0, i)
    tab = lambda i: (i % nblk, 0)
    outs = ((d, False), (MLA_HEADS * MLA_QK_PAD, False), (MLA_HEADS * MLA_QK_PAD, False),
            (MLA_HEADS * VT_ROWS, True), (GQA_HEADS * GQA_HEAD_DIM, False), (GQA_KV_HEADS * GQA_HEAD_DIM, False),
            (GQA_KV_HEADS * VT_ROWS, True), (MEM_HEADS * MEM_HEAD_DIM, False))
    weights = (gx, w_small, wvgt, gq, wqb, gkv, wkb, wvbt, gqg, gkg)
    wbytes = sum(int(w.size) * w.dtype.itemsize for w in weights)
    vmem = wbytes + 2 * tm * (d * 4 + 4 * LANES * 4 + sum(w for w, _ in outs) * 2) + tm * 4096 * 4 * 3 + (8 << 20)
    return pl.pallas_call(
        _small_kernel,
        grid=(rows // tm,),
        in_specs=[pl.BlockSpec((tm, d), row)] + [_const_spec(w.shape) for w in weights]
                 + [pl.BlockSpec((tm, LANES), tab)] * 4,
        out_specs=[pl.BlockSpec((w, tm), col) if t else pl.BlockSpec((tm, w), row) for w, t in outs],
        out_shape=[jax.ShapeDtypeStruct((w, rows) if t else (rows, w), BF16) for w, t in outs],
        compiler_params=_params(("parallel",), vmem),
        name="small_proj",
    )(x, *weights, *tabs)


def _gate_kernel(h_ref, w_ref, o_ref, *, silu):
    a = jnp.dot(h_ref[...], w_ref[...], preferred_element_type=F32)
    s = 1.0 / (1.0 + jnp.exp(-a))
    o_ref[...] = (a * s if silu else s).astype(o_ref.dtype)


def _gate(h, w, silu, tm, tn, name):
    rows, d = h.shape
    n = w.shape[1]
    vmem = 2 * (tm * d * 2 + d * tn * 2 + tm * tn * 2) + 3 * tm * tn * 4 + (8 << 20)
    return pl.pallas_call(
        functools.partial(_gate_kernel, silu=silu),
        grid=(n // tn, rows // tm),
        in_specs=[pl.BlockSpec((tm, d), lambda j, i: (i, 0)),
                  pl.BlockSpec((d, tn), lambda j, i: (0, j))],
        out_specs=pl.BlockSpec((tm, tn), lambda j, i: (i, j)),
        out_shape=jax.ShapeDtypeStruct((rows, n), BF16),
        compiler_params=_params(("parallel", "parallel"), vmem),
        name=name,
    )(h, w)


def _attn_kernel(q_ref, k_ref, vt_ref, sz_ref, o_ref, s0, s1, m0, m1, *, bq):
    nq = q_ref.shape[0] // bq
    assert nq % 2 == 0 and nq >= 2

    def rows(t):
        return pl.ds(pl.multiple_of(t * bq, bq), bq)

    def scores(t, s_scr, m_scr):
        st = lax.dot_general(k_ref[...], q_ref[rows(t), :], (((1,), (1,)), ((), ())),
                             preferred_element_type=F32)
        s_scr[:, :bq] = st
        m_scr[...] = jnp.max(st, axis=0, keepdims=True)

    def finish(t, s_scr, m_scr):
        pt = jnp.exp2(s_scr[:, :bq] - m_scr[...]).astype(BF16)
        ot = jnp.dot(vt_ref[...], pt, preferred_element_type=F32)
        o = (ot[:LANES] / ot[LANES:LANES + 1]).T
        o_ref[rows(t), :] = (o * sz_ref[rows(t), :].astype(F32)).astype(o_ref.dtype)

    scores(0, s0, m0)

    def pair(j, carry):
        t = 2 * j
        scores(t + 1, s1, m1)
        finish(t, s0, m0)
        scores(t + 2, s0, m0)
        finish(t + 1, s1, m1)
        return carry

    lax.fori_loop(0, nq // 2 - 1, pair, 0)
    scores(nq - 1, s1, m1)
    finish(nq - 2, s0, m0)
    finish(nq - 1, s1, m1)


def _attn(q, k, vt, sz, sz_block0, batch, seq, heads, kv_heads, dk, bq, name):
    rows = batch * seq
    rep = heads // kv_heads
    vmem = 2 * (2 * seq * dk * 2 + VT_ROWS * seq * 2 + 2 * seq * LANES * 2) \
        + 2 * (bq + LANES) * seq * 4 + bq * seq * (4 + 2) + (8 << 20)
    return pl.pallas_call(
        functools.partial(_attn_kernel, bq=bq),
        grid=(batch, heads),
        in_specs=[pl.BlockSpec((seq, dk), lambda b, h: (b, h)),
                  pl.BlockSpec((seq, dk), lambda b, h: (b, h // rep)),
                  pl.BlockSpec((VT_ROWS, seq), lambda b, h: (h // rep, b)),
                  pl.BlockSpec((seq, LANES), lambda b, h: (b, sz_block0 + h))],
        out_specs=pl.BlockSpec((seq, LANES), lambda b, h: (b, h)),
        out_shape=jax.ShapeDtypeStruct((rows, heads * LANES), BF16),
        scratch_shapes=[pltpu.VMEM((seq, bq + LANES), F32), pltpu.VMEM((seq, bq + LANES), F32),
                        pltpu.VMEM((1, bq), F32), pltpu.VMEM((1, bq), F32)],
        compiler_params=_params(("parallel", "parallel"), vmem),
        name=name,
    )(q, k, vt, sz)


def _memkv_kernel(m_ref, g_ref, w_ref, k_ref, v_ref):
    mn = (_rms(m_ref[...]) * g_ref[...]).astype(BF16)
    kv = jnp.dot(mn, w_ref[...], preferred_element_type=F32)
    half = kv.shape[1] // 2
    k_ref[...] = kv[:, :half].astype(BF16)
    v_ref[...] = kv[:, half:].astype(BF16)


def _memkv(mem, g, w):
    rows, d = mem.shape
    n = w.shape[1] // 2
    blk = pl.BlockSpec((MEM_TOKENS, n), lambda b: (b, 0))
    vmem = d * 2 * n * 2 + 2 * MEM_TOKENS * (d * 4 + 2 * n * 2) + MEM_TOKENS * 2 * n * 8 + (8 << 20)
    return pl.pallas_call(
        _memkv_kernel,
        grid=(rows // MEM_TOKENS,),
        in_specs=[pl.BlockSpec((MEM_TOKENS, d), lambda b: (b, 0)),
                  _const_spec(g.shape), _const_spec(w.shape)],
        out_specs=[blk, blk],
        out_shape=[jax.ShapeDtypeStruct((rows, n), BF16)] * 2,
        compiler_params=_params(("parallel",), vmem),
        name="mem_kv",
    )(mem, g, w)


def _memattn_kernel(q_ref, k_ref, v_ref, sz_ref, o_ref):
    c = LOG2E / math.sqrt(MEM_HEAD_DIM)
    for hd in range(MEM_HEADS):
        sl = slice(hd * MEM_HEAD_DIM, (hd + 1) * MEM_HEAD_DIM)
        s = lax.dot_general(q_ref[:, sl], k_ref[:, sl], (((1,), (1,)), ((), ())),
                            preferred_element_type=F32)
        m = jnp.max(s, axis=-1, keepdims=True)
        e = jnp.exp2((s - m) * c)
        l = jnp.sum(e, axis=-1, keepdims=True)
        o = jnp.dot(e.astype(BF16), v_ref[:, sl], preferred_element_type=F32) / l
        o_ref[:, sl] = (o * sz_ref[:, sl].astype(F32)).astype(o_ref.dtype)


def _memattn(qm, km, vm, sz, batch, seq, tm):
    rows, n = qm.shape
    nq = seq // tm
    tile = lambda b, i: (b * nq + i, 0)
    vmem = 2 * (3 * tm * n * 2 + 2 * MEM_TOKENS * n * 2) + 6 * tm * n * 4 + (8 << 20)
    return pl.pallas_call(
        _memattn_kernel,
        grid=(batch, nq),
        in_specs=[pl.BlockSpec((tm, n), tile),
                  pl.BlockSpec((MEM_TOKENS, n), lambda b, i: (b, 0)),
                  pl.BlockSpec((MEM_TOKENS, n), lambda b, i: (b, 0)),
                  pl.BlockSpec((tm, n), lambda b, i: (b * nq + i, N_BRANCH - 1))],
        out_specs=pl.BlockSpec((tm, n), tile),
        out_shape=jax.ShapeDtypeStruct((rows, n), BF16),
        compiler_params=_params(("parallel", "parallel"), vmem),
        name="mem_attn",
    )(qm, km, vm, sz)


def _merge_kernel(oa_ref, ob_ref, oc_ref, sg_ref, wb_ref, o_ref):
    d = o_ref.shape[1]
    acc = None
    for i, r in enumerate((oa_ref, ob_ref, oc_ref)):
        t = jnp.dot(r[...], wb_ref[i], preferred_element_type=F32)
        term = sg_ref[:, i * d:(i + 1) * d].astype(F32) * t
        acc = term if acc is None else acc + term
    o_ref[...] = acc.astype(o_ref.dtype)


def _merge(oa, ob, oc, sg, wb, tm):
    rows, n = oa.shape
    d = wb.shape[2]
    row = lambda i: (i, 0)
    vmem = int(wb.size) * 2 + 2 * tm * (3 * n * 2 + N_BRANCH * d * 2 + d * 2) + 3 * tm * d * 4 + (8 << 20)
    return pl.pallas_call(
        _merge_kernel,
        grid=(rows // tm,),
        in_specs=[pl.BlockSpec((tm, n), row)] * 3
                 + [pl.BlockSpec((tm, N_BRANCH * d), row), _const_spec(wb.shape)],
        out_specs=pl.BlockSpec((tm, d), row),
        out_shape=jax.ShapeDtypeStruct((rows, d), BF16),
        compiler_params=_params(("parallel",), vmem),
        name="merge",
    )(oa, ob, oc, sg, wb)


def _out_kernel(m_ref, x_ref, w_ref, g_ref, o_ref):
    y = x_ref[...] + jnp.dot(m_ref[...], w_ref[...], preferred_element_type=F32)
    o_ref[...] = (_rms(y) * g_ref[...]).astype(o_ref.dtype)


def _outproj(merged, x, w, g, tm):
    rows, d = x.shape
    row = lambda i: (i, 0)
    vmem = d * d * 2 + 2 * tm * d * (2 + 4 + 4) + 3 * tm * d * 4 + (8 << 20)
    return pl.pallas_call(
        _out_kernel,
        grid=(rows // tm,),
        in_specs=[pl.BlockSpec((tm, d), row), pl.BlockSpec((tm, d), row),
                  _const_spec(w.shape), _const_spec(g.shape)],
        out_specs=pl.BlockSpec((tm, d), row),
        out_shape=jax.ShapeDtypeStruct((rows, d), x.dtype),
        compiler_params=_params(("parallel",), vmem),
        name="out_proj",
    )(merged, x, w, g)


def _rope_tables(seq):
    rows = seq // GRID_W
    row = np.repeat(np.arange(rows, dtype=np.float32), GRID_W)
    col = np.tile(np.arange(GRID_W, dtype=np.float32), rows)

    def table(width):
        half = width // 4
        freqs = (ROPE_BASE ** (-np.arange(half, dtype=np.float32) / half)).astype(np.float32)
        parts_c, parts_s = [], []
        for pos in (row, col):
            ang = (pos[:, None] * freqs[None, :]).astype(np.float64)
            c, sn = np.cos(ang), np.sin(ang)
            parts_c += [c, c]
            parts_s += [-sn, sn]
        pad = ((0, 0), (0, LANES - width))
        c = np.pad(np.concatenate(parts_c, axis=-1), pad)
        sn = np.pad(np.concatenate(parts_s, axis=-1), pad)
        return jnp.asarray(c, F32), jnp.asarray(sn, F32)

    cm, sm = table(MLA_ROPE)
    cg, sg = table(GQA_HEAD_DIM)
    return cm, sm, cg, sg


def _prep_layer(w_in, w_q_b, w_kv_b, w_mem_kv, w_branch, w_out):
    d = w_in.shape[0]
    n_small = Q_LORA + KV_LORA + MLA_ROPE
    w_small = jnp.concatenate(
        [w_in[:, :n_small], jnp.zeros((d, LANES - MLA_ROPE), w_in.dtype),
         w_in[:, n_small:n_small + (_C_END - _C_QG)]], axis=1).astype(BF16)
    vg0 = n_small + (_C_VG - _C_QG)
    wvgt = w_in[:, vg0:vg0 + (_C_QM - _C_VG)].T.astype(BF16)
    z0 = n_small + (_C_END - _C_QG)
    w_z = w_in[:, z0:z0 + N_BRANCH * BRANCH_W].astype(BF16)
    w_gl = w_in[:, z0 + N_BRANCH * BRANCH_W:].astype(BF16)
    wqb = w_q_b.reshape(Q_LORA, MLA_HEADS, MLA_NOPE + MLA_ROPE)
    wqb = jnp.pad(wqb, ((0, 0), (0, 0), (0, MLA_QK_PAD - MLA_NOPE - MLA_ROPE)))
    wqb = wqb.reshape(Q_LORA, MLA_HEADS * MLA_QK_PAD).astype(BF16)
    wkv = w_kv_b.reshape(KV_LORA, MLA_HEADS, MLA_NOPE + MLA_V)
    wkb = wkv[:, :, :MLA_NOPE].reshape(KV_LORA, -1).astype(BF16)
    wvbt = wkv[:, :, MLA_NOPE:].reshape(KV_LORA, -1).T.astype(BF16)
    return (w_small, wvgt, w_z, w_gl, wqb, wkb, wvbt, w_mem_kv.astype(BF16), w_branch.astype(BF16),
            w_out.astype(BF16))


def _tile(n, pref):
    t = min(n, pref)
    assert n % t == 0, (n, t)
    return t


def _layer(x, mem, tabs, batch, seq, norm_g, mla_q_norm_g, mla_kv_norm_g, gqa_q_norm_g, gqa_k_norm_g,
           mem_norm_g, final_g, prepped):
    w_small, wvgt, w_z, w_gl, wqb, wkb, wvbt, wmem, wbr, wout = prepped
    rows = batch * seq
    r2 = lambda g: g.reshape(1, -1)
    tm = _tile(seq, 512)

    h, qmla, kmla, vmla, qg, kg, vg, qm = _small(
        x, r2(norm_g), w_small, wvgt, r2(mla_q_norm_g), wqb, r2(mla_kv_norm_g), wkb, wvbt, r2(gqa_q_norm_g),
        r2(gqa_k_norm_g), tabs, seq, tm)
    tg = _tile(rows, 1024)
    sz = _gate(h, w_z, True, tg, 1536, "gate_z")
    sg = _gate(h, w_gl, False, tg, 1536, "gate_g")

    bq = _tile(seq // 2, 512)
    o_mla = _attn(qmla, kmla, vmla, sz, 0, batch, seq, MLA_HEADS, MLA_HEADS, MLA_QK_PAD, bq, "attn_mla")
    o_gqa = _attn(qg, kg, vg, sz, BRANCH_W // LANES, batch, seq, GQA_HEADS, GQA_KV_HEADS, GQA_HEAD_DIM, bq,
                  "attn_gqa")
    km, vm = _memkv(mem, r2(mem_norm_g), wmem)
    o_mem = _memattn(qm, km, vm, sz, batch, seq, _tile(seq, 1024))

    merged = _merge(o_mla, o_gqa, o_mem, sg, wbr, tm)
    return _outproj(merged, x, wout, r2(final_g), tm)


def _trunk(x, mem, tabs, gains, prepped):
    batch, seq, d = x.shape
    y = _layer(x.reshape(batch * seq, d), mem.reshape(batch * mem.shape[1], d), tabs, batch, seq, *gains,
               prepped)
    return y.reshape(batch, seq, d)


def kernel(x_prompt, x_sample, mem_prompt, mem_sample, norm_g, w_in, mla_q_norm_g, w_q_b, mla_kv_norm_g,
           w_kv_b, gqa_q_norm_g, gqa_k_norm_g, mem_norm_g, w_mem_kv, w_branch, w_out, final_norm_g):
    assert w_in.shape[0] == 1
    first = lambda a: a.reshape(a.shape[1:])
    prepped = _prep_layer(first(w_in), first(w_q_b), first(w_kv_b), first(w_mem_kv), first(w_branch),
                          first(w_out))
    gains = (first(norm_g), first(mla_q_norm_g), first(mla_kv_norm_g), first(gqa_q_norm_g),
             first(gqa_k_norm_g), first(mem_norm_g), final_norm_g)
    assert x_prompt.shape[1] == x_sample.shape[1]
    tabs = _rope_tables(x_prompt.shape[1])
    return (_trunk(x_prompt, mem_prompt, tabs, gains, prepped),
            _trunk(x_sample, mem_sample, tabs, gains, prepped))
```

```python
import functools
import math

import jax
import jax.numpy as jnp
import numpy as np
from jax import lax
from jax.experimental import pallas as pl
from jax.experimental.pallas import tpu as pltpu

F32 = jnp.float32
BF16 = jnp.bfloat16

D_MODEL = 2048
MLA_HEADS = 8
MLA_NOPE = 128
MLA_ROPE = 64
MLA_V = 128
Q_LORA = 512
KV_LORA = 256
GQA_HEADS = 8
GQA_KV_HEADS = 2
GQA_HEAD_DIM = 128
MEM_HEADS = 4
MEM_HEAD_DIM = 256
MEM_TOKENS = 256
BRANCH_W = 1024
N_BRANCH = 3
GRID_W = 64
ROPE_BASE = 10000.0
EPS = 1e-6
LOG2E = math.log2(math.e)
MLA_SCALE = LOG2E / math.sqrt(MLA_NOPE + MLA_ROPE)
GQA_SCALE = LOG2E / math.sqrt(GQA_HEAD_DIM)

LANES = 128
MLA_QK_PAD = 2 * LANES
VT_ROWS = LANES + 16
VMEM_CAP = 60000 * 1024
GATE_PARTS = 6

_C_CQ = 0
_C_CKV = _C_CQ + Q_LORA
_C_KPE = _C_CKV + KV_LORA
_C_QG = _C_KPE + LANES
_C_KG = _C_QG + GQA_HEADS * GQA_HEAD_DIM
_C_VG = _C_KG + GQA_KV_HEADS * GQA_HEAD_DIM
_C_QM = _C_VG + GQA_KV_HEADS * GQA_HEAD_DIM
_C_END = _C_QM + MEM_HEADS * MEM_HEAD_DIM


def _params(semantics, vmem_bytes):
    return pltpu.CompilerParams(dimension_semantics=semantics,
                                vmem_limit_bytes=min(int(vmem_bytes), VMEM_CAP))


def _const_spec(shape):
    nd = len(shape)
    return pl.BlockSpec(shape, lambda *_: (0,) * nd, pipeline_mode=pl.Buffered(1))


def _rms(x):
    return x * lax.rsqrt(jnp.mean(x * x, axis=-1, keepdims=True) + EPS)


def _rope(x, cos, sin_signed, first, half):
    n = x.shape[-1]
    partner = jnp.where(first, pltpu.roll(x, n - half, 1), pltpu.roll(x, half, 1))
    return x * cos + partner * sin_signed


def _small_kernel(x_ref, gx_ref, w_ref, wvgt_ref, gq_ref, wqb_ref, gkv_ref, wkb_ref, wvbt_ref, gqg_ref, gkg_ref,
                  cosm_ref, sinm_ref, cosg_ref, sing_ref,
                  h_ref, qmla_ref, kmla_ref, vmla_ref, qg_ref, kg_ref, vg_ref, qm_ref):
    tm = x_ref.shape[0]
    h = (_rms(x_ref[...]) * gx_ref[...]).astype(BF16)
    h_ref[...] = h

    acc = jnp.dot(h, w_ref[...], preferred_element_type=F32)

    def proj(lo, hi):
        return acc[:, lo:hi]

    lane = lax.broadcasted_iota(jnp.int32, (tm, LANES), 1)
    first_m = (lane % (MLA_ROPE // 2)) < (MLA_ROPE // 4)
    first_g = (lane % (GQA_HEAD_DIM // 2)) < (GQA_HEAD_DIM // 4)
    cosm, sinm = cosm_ref[...], sinm_ref[...]
    cosg, sing = cosg_ref[...], sing_ref[...]
    ones = jnp.ones((VT_ROWS - LANES, tm), BF16)
    nt = (((1,), (1,)), ((), ()))

    cqn = (_rms(proj(_C_CQ, _C_CKV)) * gq_ref[...]).astype(BF16)
    q = jnp.dot(cqn, wqb_ref[...], preferred_element_type=F32)
    for hd in range(MLA_HEADS):
        c0 = hd * MLA_QK_PAD
        qmla_ref[:, c0:c0 + LANES] = (q[:, c0:c0 + LANES] * MLA_SCALE).astype(BF16)
        pe = _rope(q[:, c0 + LANES:c0 + 2 * LANES], cosm, sinm, first_m, MLA_ROPE // 4)
        qmla_ref[:, c0 + LANES:c0 + 2 * LANES] = (pe * MLA_SCALE).astype(BF16)

    ckvn = (_rms(proj(_C_CKV, _C_KPE)) * gkv_ref[...]).astype(BF16)
    kn = jnp.dot(ckvn, wkb_ref[...], preferred_element_type=F32)
    vt = lax.dot_general(wvbt_ref[...], ckvn, nt, preferred_element_type=F32)
    kpe = _rope(proj(_C_KPE, _C_QG), cosm, sinm, first_m, MLA_ROPE // 4).astype(BF16)
    for hd in range(MLA_HEADS):
        c0 = hd * MLA_QK_PAD
        kmla_ref[:, c0:c0 + LANES] = kn[:, hd * MLA_NOPE:(hd + 1) * MLA_NOPE].astype(BF16)
        kmla_ref[:, c0 + LANES:c0 + 2 * LANES] = kpe
        v0 = hd * VT_ROWS
        vmla_ref[v0:v0 + LANES, :] = vt[hd * MLA_V:(hd + 1) * MLA_V].astype(BF16)
        vmla_ref[v0 + LANES:v0 + VT_ROWS, :] = ones

    qg = proj(_C_QG, _C_KG)
    for hd in range(GQA_HEADS):
        sl = slice(hd * GQA_HEAD_DIM, (hd + 1) * GQA_HEAD_DIM)
        xn = _rms(qg[:, sl]) * gqg_ref[...]
        qg_ref[:, sl] = (_rope(xn, cosg, sing, first_g, GQA_HEAD_DIM // 4) * GQA_SCALE).astype(BF16)
    kg = proj(_C_KG, _C_VG)
    vgt = lax.dot_general(wvgt_ref[...], h, nt, preferred_element_type=F32)
    for hd in range(GQA_KV_HEADS):
        sl = slice(hd * GQA_HEAD_DIM, (hd + 1) * GQA_HEAD_DIM)
        xn = _rms(kg[:, sl]) * gkg_ref[...]
        kg_ref[:, sl] = _rope(xn, cosg, sing, first_g, GQA_HEAD_DIM // 4).astype(BF16)
        v0 = hd * VT_ROWS
        vg_ref[v0:v0 + LANES, :] = vgt[sl].astype(BF16)
        vg_ref[v0 + LANES:v0 + VT_ROWS, :] = ones

    qm_ref[...] = proj(_C_QM, _C_END).astype(BF16)


def _small(x, gx, w_small, wvgt, gq, wqb, gkv, wkb, wvbt, gqg, gkg, tabs, seq, tm):
    rows, d = x.shape
    nblk = seq // tm
    row = lambda i: (i, 0)
    col = lambda i: (0, i)
    tab = lambda i: (i % nblk, 0)
    outs = ((d, False), (MLA_HEADS * MLA_QK_PAD, False), (MLA_HEADS * MLA_QK_PAD, False),
            (MLA_HEADS * VT_ROWS, True), (GQA_HEADS * GQA_HEAD_DIM, False), (GQA_KV_HEADS * GQA_HEAD_DIM, False),
            (GQA_KV_HEADS * VT_ROWS, True), (MEM_HEADS * MEM_HEAD_DIM, False))
    weights = (gx, w_small, wvgt, gq, wqb, gkv, wkb, wvbt, gqg, gkg)
    wbytes = sum(int(w.size) * w.dtype.itemsize for w in weights)
    vmem = wbytes + 2 * tm * (d * 4 + 4 * LANES * 4 + sum(w for w, _ in outs) * 2) + tm * 4096 * 4 * 3 + (8 << 20)
    return pl.pallas_call(
        _small_kernel,
        grid=(rows // tm,),
        in_specs=[pl.BlockSpec((tm, d), row)] + [_const_spec(w.shape) for w in weights]
                 + [pl.BlockSpec((tm, LANES), tab)] * 4,
        out_specs=[pl.BlockSpec((w, tm), col) if t else pl.BlockSpec((tm, w), row) for w, t in outs],
        out_shape=[jax.ShapeDtypeStruct((w, rows) if t else (rows, w), BF16) for w, t in outs],
        compiler_params=_params(("parallel",), vmem),
        name="small_proj",
    )(x, *weights, *tabs)


def _gate_kernel(h_ref, w_ref, o_ref, *, silu):
    h = h_ref[...]
    step = w_ref.shape[1] // GATE_PARTS
    for p in range(GATE_PARTS):
        a = jnp.dot(h, w_ref[:, p * step:(p + 1) * step], preferred_element_type=F32)
        s = 0.5 + 0.5 * jnp.tanh(0.5 * a)
        o_ref[:, p * step:(p + 1) * step] = (a * s if silu else s).astype(o_ref.dtype)


def _gate(h, w, silu, tm, tn, name):
    rows, d = h.shape
    n = w.shape[1]
    assert tn % (GATE_PARTS * LANES) == 0
    vmem = 2 * (tm * d * 2 + d * tn * 2 + tm * tn * 2) + 3 * tm * tn * 4 + (8 << 20)
    return pl.pallas_call(
        functools.partial(_gate_kernel, silu=silu),
        grid=(n // tn, rows // tm),
        in_specs=[pl.BlockSpec((tm, d), lambda j, i: (i, 0)),
                  pl.BlockSpec((d, tn), lambda j, i: (0, j))],
        out_specs=pl.BlockSpec((tm, tn), lambda j, i: (i, j)),
        out_shape=jax.ShapeDtypeStruct((rows, n), BF16),
        compiler_params=_params(("parallel", "parallel"), vmem),
        name=name,
    )(h, w)


def _attn_kernel(q_ref, k_ref, qn_ref, kn_ref, vt_ref, sz_ref, o_ref, s0, s1, m0, m1, *, bq):
    nq = q_ref.shape[0] // bq
    assert nq % 2 == 0 and nq >= 2

    def rows(t):
        return pl.ds(pl.multiple_of(t * bq, bq), bq)

    def scores(q, k_r, s_scr, m_scr):
        st = lax.dot_general(k_r[...], q, (((1,), (1,)), ((), ())), preferred_element_type=F32)
        s_scr[...] = st
        m_scr[...] = jnp.max(st, axis=0, keepdims=True)

    def finish(t, s_scr, m_scr):
        pt = jnp.exp2(s_scr[...] - m_scr[...]).astype(BF16)
        ot = jnp.dot(vt_ref[...], pt, preferred_element_type=F32)
        o = (ot[:LANES] / ot[LANES:LANES + 1]).T
        o_ref[rows(t), :] = (o * sz_ref[rows(t), :].astype(F32)).astype(o_ref.dtype)

    @pl.when((pl.program_id(0) == 0) & (pl.program_id(1) == 0))
    def _():
        scores(q_ref[rows(0), :], k_ref, s0, m0)

    def pair(j, carry):
        t = 2 * j
        scores(q_ref[rows(t + 1), :], k_ref, s1, m1)
        finish(t, s0, m0)
        scores(q_ref[rows(t + 2), :], k_ref, s0, m0)
        finish(t + 1, s1, m1)
        return carry

    lax.fori_loop(0, nq // 2 - 1, pair, 0)
    scores(q_ref[rows(nq - 1), :], k_ref, s1, m1)
    finish(nq - 2, s0, m0)
    scores(qn_ref[...], kn_ref, s0, m0)
    finish(nq - 1, s1, m1)


def _attn(q, k, vt, sz, sz_block0, batch, seq, heads, kv_heads, dk, bq, name):
    rows = batch * seq
    rep = heads // kv_heads
    nq = seq // bq

    def nxt(b, h):
        wrap = (h + 1) // heads
        return jnp.minimum(b + wrap, batch - 1), (h + 1) % heads

    def qn_map(b, h):
        nb, nh = nxt(b, h)
        return nb * nq, nh

    def kn_map(b, h):
        nb, nh = nxt(b, h)
        return nb, nh // rep

    vmem = 2 * (3 * seq * dk * 2 + bq * dk * 2 + VT_ROWS * seq * 2 + 2 * seq * LANES * 2) \
        + bq * seq * (2 * 4 + 4 + 2) + (8 << 20)
    return pl.pallas_call(
        functools.partial(_attn_kernel, bq=bq),
        grid=(batch, heads),
        in_specs=[pl.BlockSpec((seq, dk), lambda b, h: (b, h)),
                  pl.BlockSpec((seq, dk), lambda b, h: (b, h // rep)),
                  pl.BlockSpec((bq, dk), qn_map),
                  pl.BlockSpec((seq, dk), kn_map),
                  pl.BlockSpec((VT_ROWS, seq), lambda b, h: (h // rep, b)),
                  pl.BlockSpec((seq, LANES), lambda b, h: (b, sz_block0 + h))],
        out_specs=pl.BlockSpec((seq, LANES), lambda b, h: (b, h)),
        out_shape=jax.ShapeDtypeStruct((rows, heads * LANES), BF16),
        scratch_shapes=[pltpu.VMEM((seq, bq), F32), pltpu.VMEM((seq, bq), F32),
                        pltpu.VMEM((1, bq), F32), pltpu.VMEM((1, bq), F32)],
        compiler_params=_params(("arbitrary", "arbitrary"), vmem),
        name=name,
    )(q, k, q, k, vt, sz)


def _memkv_kernel(m_ref, g_ref, w_ref, k_ref, v_ref):
    mn = (_rms(m_ref[...]) * g_ref[...]).astype(BF16)
    kv = jnp.dot(mn, w_ref[...], preferred_element_type=F32)
    half = kv.shape[1] // 2
    k_ref[...] = kv[:, :half].astype(BF16)
    v_ref[...] = kv[:, half:].astype(BF16)


def _memkv(mem, g, w):
    rows, d = mem.shape
    n = w.shape[1] // 2
    blk = pl.BlockSpec((MEM_TOKENS, n), lambda b: (b, 0))
    vmem = d * 2 * n * 2 + 2 * MEM_TOKENS * (d * 4 + 2 * n * 2) + MEM_TOKENS * 2 * n * 8 + (8 << 20)
    return pl.pallas_call(
        _memkv_kernel,
        grid=(rows // MEM_TOKENS,),
        in_specs=[pl.BlockSpec((MEM_TOKENS, d), lambda b: (b, 0)),
                  _const_spec(g.shape), _const_spec(w.shape)],
        out_specs=[blk, blk],
        out_shape=[jax.ShapeDtypeStruct((rows, n), BF16)] * 2,
        compiler_params=_params(("parallel",), vmem),
        name="mem_kv",
    )(mem, g, w)


def _memattn_kernel(q_ref, k_ref, v_ref, sz_ref, o_ref):
    c = LOG2E / math.sqrt(MEM_HEAD_DIM)
    for hd in range(MEM_HEADS):
        sl = slice(hd * MEM_HEAD_DIM, (hd + 1) * MEM_HEAD_DIM)
        s = lax.dot_general(q_ref[:, sl], k_ref[:, sl], (((1,), (1,)), ((), ())),
                            preferred_element_type=F32)
        m = jnp.max(s, axis=-1, keepdims=True)
        e = jnp.exp2((s - m) * c)
        l = jnp.sum(e, axis=-1, keepdims=True)
        o = jnp.dot(e.astype(BF16), v_ref[:, sl], preferred_element_type=F32) / l
        o_ref[:, sl] = (o * sz_ref[:, sl].astype(F32)).astype(o_ref.dtype)


def _memattn(qm, km, vm, sz, batch, seq, tm):
    rows, n = qm.shape
    nq = seq // tm
    tile = lambda b, i: (b * nq + i, 0)
    vmem = 2 * (3 * tm * n * 2 + 2 * MEM_TOKENS * n * 2) + 6 * tm * n * 4 + (8 << 20)
    return pl.pallas_call(
        _memattn_kernel,
        grid=(batch, nq),
        in_specs=[pl.BlockSpec((tm, n), tile),
                  pl.BlockSpec((MEM_TOKENS, n), lambda b, i: (b, 0)),
                  pl.BlockSpec((MEM_TOKENS, n), lambda b, i: (b, 0)),
                  pl.BlockSpec((tm, n), lambda b, i: (b * nq + i, N_BRANCH - 1))],
        out_specs=pl.BlockSpec((tm, n), tile),
        out_shape=jax.ShapeDtypeStruct((rows, n), BF16),
        compiler_params=_params(("parallel", "parallel"), vmem),
        name="mem_attn",
    )(qm, km, vm, sz)


def _merge_kernel(oa_ref, ob_ref, oc_ref, sg_ref, wb_ref, o_ref):
    d = o_ref.shape[1]
    acc = None
    for i, r in enumerate((oa_ref, ob_ref, oc_ref)):
        t = jnp.dot(r[...], wb_ref[i], preferred_element_type=F32)
        term = sg_ref[:, i * d:(i + 1) * d].astype(F32) * t
        acc = term if acc is None else acc + term
    o_ref[...] = acc.astype(o_ref.dtype)


def _merge(oa, ob, oc, sg, wb, tm):
    rows, n = oa.shape
    d = wb.shape[2]
    row = lambda i: (i, 0)
    vmem = int(wb.size) * 2 + 2 * tm * (3 * n * 2 + N_BRANCH * d * 2 + d * 2) + 3 * tm * d * 4 + (8 << 20)
    return pl.pallas_call(
        _merge_kernel,
        grid=(rows // tm,),
        in_specs=[pl.BlockSpec((tm, n), row)] * 3
                 + [pl.BlockSpec((tm, N_BRANCH * d), row), _const_spec(wb.shape)],
        out_specs=pl.BlockSpec((tm, d), row),
        out_shape=jax.ShapeDtypeStruct((rows, d), BF16),
        compiler_params=_params(("parallel",), vmem),
        name="merge",
    )(oa, ob, oc, sg, wb)


def _out_kernel(m_ref, x_ref, w_ref, g_ref, o_ref):
    y = x_ref[...] + jnp.dot(m_ref[...], w_ref[...], preferred_element_type=F32)
    o_ref[...] = (_rms(y) * g_ref[...]).astype(o_ref.dtype)


def _outproj(merged, x, w, g, tm):
    rows, d = x.shape
    row = lambda i: (i, 0)
    vmem = d * d * 2 + 2 * tm * d * (2 + 4 + 4) + 3 * tm * d * 4 + (8 << 20)
    return pl.pallas_call(
        _out_kernel,
        grid=(rows // tm,),
        in_specs=[pl.BlockSpec((tm, d), row), pl.BlockSpec((tm, d), row),
                  _const_spec(w.shape), _const_spec(g.shape)],
        out_specs=pl.BlockSpec((tm, d), row),
        out_shape=jax.ShapeDtypeStruct((rows, d), x.dtype),
        compiler_params=_params(("parallel",), vmem),
        name="out_proj",
    )(merged, x, w, g)


def _rope_tables(seq):
    rows = seq // GRID_W
    row = np.repeat(np.arange(rows, dtype=np.float32), GRID_W)
    col = np.tile(np.arange(GRID_W, dtype=np.float32), rows)

    def table(width):
        half = width // 4
        freqs = (ROPE_BASE ** (-np.arange(half, dtype=np.float32) / half)).astype(np.float32)
        parts_c, parts_s = [], []
        for pos in (row, col):
            ang = (pos[:, None] * freqs[None, :]).astype(np.float64)
            c, sn = np.cos(ang), np.sin(ang)
            parts_c += [c, c]
            parts_s += [-sn, sn]
        pad = ((0, 0), (0, LANES - width))
        c = np.pad(np.concatenate(parts_c, axis=-1), pad)
        sn = np.pad(np.concatenate(parts_s, axis=-1), pad)
        return jnp.asarray(c, F32), jnp.asarray(sn, F32)

    cm, sm = table(MLA_ROPE)
    cg, sg = table(GQA_HEAD_DIM)
    return cm, sm, cg, sg


def _prep_layer(w_in, w_q_b, w_kv_b, w_mem_kv, w_branch, w_out):
    d = w_in.shape[0]
    n_small = Q_LORA + KV_LORA + MLA_ROPE
    w_small = jnp.concatenate(
        [w_in[:, :n_small], jnp.zeros((d, LANES - MLA_ROPE), w_in.dtype),
         w_in[:, n_small:n_small + (_C_END - _C_QG)]], axis=1).astype(BF16)
    vg0 = n_small + (_C_VG - _C_QG)
    wvgt = w_in[:, vg0:vg0 + (_C_QM - _C_VG)].T.astype(BF16)
    z0 = n_small + (_C_END - _C_QG)
    w_z = w_in[:, z0:z0 + N_BRANCH * BRANCH_W].astype(BF16)
    w_gl = w_in[:, z0 + N_BRANCH * BRANCH_W:].astype(BF16)
    wqb = w_q_b.reshape(Q_LORA, MLA_HEADS, MLA_NOPE + MLA_ROPE)
    wqb = jnp.pad(wqb, ((0, 0), (0, 0), (0, MLA_QK_PAD - MLA_NOPE - MLA_ROPE)))
    wqb = wqb.reshape(Q_LORA, MLA_HEADS * MLA_QK_PAD).astype(BF16)
    wkv = w_kv_b.reshape(KV_LORA, MLA_HEADS, MLA_NOPE + MLA_V)
    wkb = wkv[:, :, :MLA_NOPE].reshape(KV_LORA, -1).astype(BF16)
    wvbt = wkv[:, :, MLA_NOPE:].reshape(KV_LORA, -1).T.astype(BF16)
    return (w_small, wvgt, w_z, w_gl, wqb, wkb, wvbt, w_mem_kv.astype(BF16), w_branch.astype(BF16),
            w_out.astype(BF16))


def _tile(n, pref):
    t = min(n, pref)
    assert n % t == 0, (n, t)
    return t


def _layer(x, mem, tabs, batch, seq, norm_g, mla_q_norm_g, mla_kv_norm_g, gqa_q_norm_g, gqa_k_norm_g,
           mem_norm_g, final_g, prepped):
    w_small, wvgt, w_z, w_gl, wqb, wkb, wvbt, wmem, wbr, wout = prepped
    rows = batch * seq
    r2 = lambda g: g.reshape(1, -1)
    tm = _tile(seq, 512)

    h, qmla, kmla, vmla, qg, kg, vg, qm = _small(
        x, r2(norm_g), w_small, wvgt, r2(mla_q_norm_g), wqb, r2(mla_kv_norm_g), wkb, wvbt, r2(gqa_q_norm_g),
        r2(gqa_k_norm_g), tabs, seq, tm)
    tg = _tile(rows, 1024)
    sz = _gate(h, w_z, True, tg, 1536, "gate_z")
    sg = _gate(h, w_gl, False, tg, 1536, "gate_g")

    bq = _tile(seq // 2, 512)
    o_mla = _attn(qmla, kmla, vmla, sz, 0, batch, seq, MLA_HEADS, MLA_HEADS, MLA_QK_PAD, bq, "attn_mla")
    o_gqa = _attn(qg, kg, vg, sz, BRANCH_W // LANES, batch, seq, GQA_HEADS, GQA_KV_HEADS, GQA_HEAD_DIM, bq,
                  "attn_gqa")
    km, vm = _memkv(mem, r2(mem_norm_g), wmem)
    o_mem = _memattn(qm, km, vm, sz, batch, seq, _tile(seq, 1024))

    merged = _merge(o_mla, o_gqa, o_mem, sg, wbr, tm)
    return _outproj(merged, x, wout, r2(final_g), tm)


def _trunk(x, mem, tabs, gains, prepped):
    batch, seq, d = x.shape
    y = _layer(x.reshape(batch * seq, d), mem.reshape(batch * mem.shape[1], d), tabs, batch, seq, *gains,
               prepped)
    return y.reshape(batch, seq, d)


def kernel(x_prompt, x_sample, mem_prompt, mem_sample, norm_g, w_in, mla_q_norm_g, w_q_b, mla_kv_norm_g,
           w_kv_b, gqa_q_norm_g, gqa_k_norm_g, mem_norm_g, w_mem_kv, w_branch, w_out, final_norm_g):
    assert w_in.shape[0] == 1
    first = lambda a: a.reshape(a.shape[1:])
    prepped = _prep_layer(first(w_in), first(w_q_b), first(w_kv_b), first(w_mem_kv), first(w_branch),
                          first(w_out))
    gains = (first(norm_g), first(mla_q_norm_g), first(mla_kv_norm_g), first(gqa_q_norm_g),
             first(gqa_k_norm_g), first(mem_norm_g), final_norm_g)
    assert x_prompt.shape[1] == x_sample.shape[1]
    tabs = _rope_tables(x_prompt.shape[1])
    return (_trunk(x_prompt, mem_prompt, tabs, gains, prepped),
            _trunk(x_sample, mem_sample, tabs, gains, prepped))
```

```python
import functools
import math

import jax
import jax.numpy as jnp
import numpy as np
from jax import lax
from jax.experimental import pallas as pl
from jax.experimental.pallas import tpu as pltpu

F32 = jnp.float32
BF16 = jnp.bfloat16

D_MODEL = 2048
MLA_HEADS = 8
MLA_NOPE = 128
MLA_ROPE = 64
MLA_V = 128
Q_LORA = 512
KV_LORA = 256
GQA_HEADS = 8
GQA_KV_HEADS = 2
GQA_HEAD_DIM = 128
MEM_HEADS = 4
MEM_HEAD_DIM = 256
MEM_TOKENS = 256
BRANCH_W = 1024
N_BRANCH = 3
GRID_W = 64
ROPE_BASE = 10000.0
EPS = 1e-6
LOG2E = math.log2(math.e)
MLA_SCALE = LOG2E / math.sqrt(MLA_NOPE + MLA_ROPE)
GQA_SCALE = LOG2E / math.sqrt(GQA_HEAD_DIM)

LANES = 128
MLA_QK_PAD = 2 * LANES
VT_ROWS = LANES + 16
VMEM_CAP = 60000 * 1024
GATE_PARTS = 6

_C_CQ = 0
_C_CKV = _C_CQ + Q_LORA
_C_KPE = _C_CKV + KV_LORA
_C_QG = _C_KPE + LANES
_C_KG = _C_QG + GQA_HEADS * GQA_HEAD_DIM
_C_VG = _C_KG + GQA_KV_HEADS * GQA_HEAD_DIM
_C_QM = _C_VG + GQA_KV_HEADS * GQA_HEAD_DIM
_C_END = _C_QM + MEM_HEADS * MEM_HEAD_DIM


def _params(semantics, vmem_bytes):
    return pltpu.CompilerParams(dimension_semantics=semantics,
                                vmem_limit_bytes=min(int(vmem_bytes), VMEM_CAP))


def _const_spec(shape):
    nd = len(shape)
    return pl.BlockSpec(shape, lambda *_: (0,) * nd, pipeline_mode=pl.Buffered(1))


def _rms(x):
    return x * lax.rsqrt(jnp.mean(x * x, axis=-1, keepdims=True) + EPS)


def _rope(x, cos, sin_signed, first, half):
    n = x.shape[-1]
    partner = jnp.where(first, pltpu.roll(x, n - half, 1), pltpu.roll(x, half, 1))
    return x * cos + partner * sin_signed


def _small_kernel(x_ref, gx_ref, w_ref, wvgt_ref, gq_ref, wqb_ref, gkv_ref, wkb_ref, wvbt_ref, gqg_ref, gkg_ref,
                  cosm_ref, sinm_ref, cosg_ref, sing_ref,
                  h_ref, qmla_ref, kmla_ref, vmla_ref, qg_ref, kg_ref, vg_ref, qm_ref):
    tm = x_ref.shape[0]
    h = (_rms(x_ref[...]) * gx_ref[...]).astype(BF16)
    h_ref[...] = h

    acc = jnp.dot(h, w_ref[...], preferred_element_type=F32)

    def proj(lo, hi):
        return acc[:, lo:hi]

    lane = lax.broadcasted_iota(jnp.int32, (tm, LANES), 1)
    first_m = (lane % (MLA_ROPE // 2)) < (MLA_ROPE // 4)
    first_g = (lane % (GQA_HEAD_DIM // 2)) < (GQA_HEAD_DIM // 4)
    cosm, sinm = cosm_ref[...], sinm_ref[...]
    cosg, sing = cosg_ref[...], sing_ref[...]
    ones = jnp.ones((VT_ROWS - LANES, tm), BF16)
    nt = (((1,), (1,)), ((), ()))

    cqn = (_rms(proj(_C_CQ, _C_CKV)) * gq_ref[...]).astype(BF16)
    q = jnp.dot(cqn, wqb_ref[...], preferred_element_type=F32)
    for hd in range(MLA_HEADS):
        c0 = hd * MLA_QK_PAD
        qmla_ref[:, c0:c0 + LANES] = (q[:, c0:c0 + LANES] * MLA_SCALE).astype(BF16)
        pe = _rope(q[:, c0 + LANES:c0 + 2 * LANES], cosm, sinm, first_m, MLA_ROPE // 4)
        qmla_ref[:, c0 + LANES:c0 + 2 * LANES] = (pe * MLA_SCALE).astype(BF16)

    ckvn = (_rms(proj(_C_CKV, _C_KPE)) * gkv_ref[...]).astype(BF16)
    kn = jnp.dot(ckvn, wkb_ref[...], preferred_element_type=F32)
    vt = lax.dot_general(wvbt_ref[...], ckvn, nt, preferred_element_type=F32)
    kpe = _rope(proj(_C_KPE, _C_QG), cosm, sinm, first_m, MLA_ROPE // 4).astype(BF16)
    for hd in range(MLA_HEADS):
        c0 = hd * MLA_QK_PAD
        kmla_ref[:, c0:c0 + LANES] = kn[:, hd * MLA_NOPE:(hd + 1) * MLA_NOPE].astype(BF16)
        kmla_ref[:, c0 + LANES:c0 + 2 * LANES] = kpe
        v0 = hd * VT_ROWS
        vmla_ref[v0:v0 + LANES, :] = vt[hd * MLA_V:(hd + 1) * MLA_V].astype(BF16)
        vmla_ref[v0 + LANES:v0 + VT_ROWS, :] = ones

    qg = proj(_C_QG, _C_KG)
    for hd in range(GQA_HEADS):
        sl = slice(hd * GQA_HEAD_DIM, (hd + 1) * GQA_HEAD_DIM)
        xn = _rms(qg[:, sl]) * gqg_ref[...]
        qg_ref[:, sl] = (_rope(xn, cosg, sing, first_g, GQA_HEAD_DIM // 4) * GQA_SCALE).astype(BF16)
    kg = proj(_C_KG, _C_VG)
    vgt = lax.dot_general(wvgt_ref[...], h, nt, preferred_element_type=F32)
    for hd in range(GQA_KV_HEADS):
        sl = slice(hd * GQA_HEAD_DIM, (hd + 1) * GQA_HEAD_DIM)
        xn = _rms(kg[:, sl]) * gkg_ref[...]
        kg_ref[:, sl] = _rope(xn, cosg, sing, first_g, GQA_HEAD_DIM // 4).astype(BF16)
        v0 = hd * VT_ROWS
        vg_ref[v0:v0 + LANES, :] = vgt[sl].astype(BF16)
        vg_ref[v0 + LANES:v0 + VT_ROWS, :] = ones

    qm_ref[...] = proj(_C_QM, _C_END).astype(BF16)


def _small(x, gx, w_small, wvgt, gq, wqb, gkv, wkb, wvbt, gqg, gkg, tabs, seq, tm):
    rows, d = x.shape
    nblk = seq // tm
    row = lambda i: (i, 0)
    col = lambda i: (0, i)
    tab = lambda i: (i % nblk, 0)
    outs = ((d, False), (MLA_HEADS * MLA_QK_PAD, False), (MLA_HEADS * MLA_QK_PAD, False),
            (MLA_HEADS * VT_ROWS, True), (GQA_HEADS * GQA_HEAD_DIM, False), (GQA_KV_HEADS * GQA_HEAD_DIM, False),
            (GQA_KV_HEADS * VT_ROWS, True), (MEM_HEADS * MEM_HEAD_DIM, False))
    weights = (gx, w_small, wvgt, gq, wqb, gkv, wkb, wvbt, gqg, gkg)
    wbytes = sum(int(w.size) * w.dtype.itemsize for w in weights)
    vmem = wbytes + 2 * tm * (d * 4 + 4 * LANES * 4 + sum(w for w, _ in outs) * 2) + tm * 4096 * 4 * 3 + (8 << 20)
    return pl.pallas_call(
        _small_kernel,
        grid=(rows // tm,),
        in_specs=[pl.BlockSpec((tm, d), row)] + [_const_spec(w.shape) for w in weights]
                 + [pl.BlockSpec((tm, LANES), tab)] * 4,
        out_specs=[pl.BlockSpec((w, tm), col) if t else pl.BlockSpec((tm, w), row) for w, t in outs],
        out_shape=[jax.ShapeDtypeStruct((w, rows) if t else (rows, w), BF16) for w, t in outs],
        compiler_params=_params(("parallel",), vmem),
        name="small_proj",
    )(x, *weights, *tabs)


def _gate_kernel(h_ref, w_ref, o_ref, *, silu):
    h = h_ref[...]
    step = w_ref.shape[1] // GATE_PARTS
    for p in range(GATE_PARTS):
        a = jnp.dot(h, w_ref[:, p * step:(p + 1) * step], preferred_element_type=F32)
        s = 0.5 + 0.5 * jnp.tanh(0.5 * a)
        o_ref[:, p * step:(p + 1) * step] = (a * s if silu else s).astype(o_ref.dtype)


def _gate(h, w, silu, tm, tn, name):
    rows, d = h.shape
    n = w.shape[1]
    assert tn % (GATE_PARTS * LANES) == 0
    vmem = 2 * (tm * d * 2 + d * tn * 2 + tm * tn * 2) + 3 * tm * tn * 4 + (8 << 20)
    return pl.pallas_call(
        functools.partial(_gate_kernel, silu=silu),
        grid=(n // tn, rows // tm),
        in_specs=[pl.BlockSpec((tm, d), lambda j, i: (i, 0)),
                  pl.BlockSpec((d, tn), lambda j, i: (0, j))],
        out_specs=pl.BlockSpec((tm, tn), lambda j, i: (i, j)),
        out_shape=jax.ShapeDtypeStruct((rows, n), BF16),
        compiler_params=_params(("parallel", "parallel"), vmem),
        name=name,
    )(h, w)


def _attn_kernel(q_ref, k_ref, qn_ref, kn_ref, vt_ref, sz_ref, o_ref, s0, s1, m0, m1, *, bq):
    nq = q_ref.shape[0] // bq
    assert nq % 2 == 0 and nq >= 2

    def rows(t):
        return pl.ds(pl.multiple_of(t * bq, bq), bq)

    def scores(q, k_r, s_scr, m_scr):
        st = lax.dot_general(k_r[...], q, (((1,), (1,)), ((), ())), preferred_element_type=F32)
        s_scr[...] = st
        m_scr[...] = jnp.max(st, axis=0, keepdims=True)

    def finish(t, s_scr, m_scr):
        pt = jnp.exp2(s_scr[...] - m_scr[...]).astype(BF16)
        ot = jnp.dot(vt_ref[...], pt, preferred_element_type=F32)
        o = (ot[:LANES] / ot[LANES:LANES + 1]).T
        o_ref[rows(t), :] = (o * sz_ref[rows(t), :].astype(F32)).astype(o_ref.dtype)

    @pl.when((pl.program_id(0) == 0) & (pl.program_id(1) == 0))
    def _():
        scores(q_ref[rows(0), :], k_ref, s0, m0)

    def pair(j, carry):
        t = 2 * j
        scores(q_ref[rows(t + 1), :], k_ref, s1, m1)
        finish(t, s0, m0)
        scores(q_ref[rows(t + 2), :], k_ref, s0, m0)
        finish(t + 1, s1, m1)
        return carry

    lax.fori_loop(0, nq // 2 - 1, pair, 0)
    scores(q_ref[rows(nq - 1), :], k_ref, s1, m1)
    finish(nq - 2, s0, m0)
    scores(qn_ref[...], kn_ref, s0, m0)
    finish(nq - 1, s1, m1)


def _attn(q, k, vt, sz, sz_block0, batch, seq, heads, kv_heads, dk, bq, name):
    rows = batch * seq
    rep = heads // kv_heads
    nq = seq // bq

    def nxt(b, h):
        wrap = (h + 1) // heads
        return jnp.minimum(b + wrap, batch - 1), (h + 1) % heads

    def qn_map(b, h):
        nb, nh = nxt(b, h)
        return nb * nq, nh

    def kn_map(b, h):
        nb, nh = nxt(b, h)
        return nb, nh // rep

    vmem = 2 * (3 * seq * dk * 2 + bq * dk * 2 + VT_ROWS * seq * 2 + 2 * seq * LANES * 2) \
        + bq * seq * (2 * 4 + 4 + 2) + (8 << 20)
    return pl.pallas_call(
        functools.partial(_attn_kernel, bq=bq),
        grid=(batch, heads),
        in_specs=[pl.BlockSpec((seq, dk), lambda b, h: (b, h)),
                  pl.BlockSpec((seq, dk), lambda b, h: (b, h // rep)),
                  pl.BlockSpec((bq, dk), qn_map),
                  pl.BlockSpec((seq, dk), kn_map),
                  pl.BlockSpec((VT_ROWS, seq), lambda b, h: (h // rep, b)),
                  pl.BlockSpec((seq, LANES), lambda b, h: (b, sz_block0 + h))],
        out_specs=pl.BlockSpec((seq, LANES), lambda b, h: (b, h)),
        out_shape=jax.ShapeDtypeStruct((rows, heads * LANES), BF16),
        scratch_shapes=[pltpu.VMEM((seq, bq), F32), pltpu.VMEM((seq, bq), F32),
                        pltpu.VMEM((1, bq), F32), pltpu.VMEM((1, bq), F32)],
        compiler_params=_params(("arbitrary", "arbitrary"), vmem),
        name=name,
    )(q, k, q, k, vt, sz)


def _memkv_kernel(m_ref, g_ref, w_ref, k_ref, v_ref):
    mn = (_rms(m_ref[...]) * g_ref[...]).astype(BF16)
    kv = jnp.dot(mn, w_ref[...], preferred_element_type=F32)
    half = kv.shape[1] // 2
    k_ref[...] = kv[:, :half].astype(BF16)
    v_ref[...] = kv[:, half:].astype(BF16)


def _memkv(mem, g, w):
    rows, d = mem.shape
    n = w.shape[1] // 2
    blk = pl.BlockSpec((MEM_TOKENS, n), lambda b: (b, 0))
    vmem = d * 2 * n * 2 + 2 * MEM_TOKENS * (d * 4 + 2 * n * 2) + MEM_TOKENS * 2 * n * 8 + (8 << 20)
    return pl.pallas_call(
        _memkv_kernel,
        grid=(rows // MEM_TOKENS,),
        in_specs=[pl.BlockSpec((MEM_TOKENS, d), lambda b: (b, 0)),
                  _const_spec(g.shape), _const_spec(w.shape)],
        out_specs=[blk, blk],
        out_shape=[jax.ShapeDtypeStruct((rows, n), BF16)] * 2,
        compiler_params=_params(("parallel",), vmem),
        name="mem_kv",
    )(mem, g, w)


def _memattn_kernel(q_ref, k_ref, v_ref, sz_ref, o_ref):
    c = LOG2E / math.sqrt(MEM_HEAD_DIM)
    for hd in range(MEM_HEADS):
        sl = slice(hd * MEM_HEAD_DIM, (hd + 1) * MEM_HEAD_DIM)
        s = lax.dot_general(q_ref[:, sl], k_ref[:, sl], (((1,), (1,)), ((), ())),
                            preferred_element_type=F32)
        m = jnp.max(s, axis=-1, keepdims=True)
        e = jnp.exp2((s - m) * c)
        l = jnp.sum(e, axis=-1, keepdims=True)
        o = jnp.dot(e.astype(BF16), v_ref[:, sl], preferred_element_type=F32) / l
        o_ref[:, sl] = (o * sz_ref[:, sl].astype(F32)).astype(o_ref.dtype)


def _memattn(qm, km, vm, sz, batch, seq, tm):
    rows, n = qm.shape
    nq = seq // tm
    tile = lambda b, i: (b * nq + i, 0)
    vmem = 2 * (3 * tm * n * 2 + 2 * MEM_TOKENS * n * 2) + 6 * tm * n * 4 + (8 << 20)
    return pl.pallas_call(
        _memattn_kernel,
        grid=(batch, nq),
        in_specs=[pl.BlockSpec((tm, n), tile),
                  pl.BlockSpec((MEM_TOKENS, n), lambda b, i: (b, 0)),
                  pl.BlockSpec((MEM_TOKENS, n), lambda b, i: (b, 0)),
                  pl.BlockSpec((tm, n), lambda b, i: (b * nq + i, N_BRANCH - 1))],
        out_specs=pl.BlockSpec((tm, n), tile),
        out_shape=jax.ShapeDtypeStruct((rows, n), BF16),
        compiler_params=_params(("parallel", "parallel"), vmem),
        name="mem_attn",
    )(qm, km, vm, sz)


def _merge_kernel(oa_ref, ob_ref, oc_ref, sg_ref, wb_ref, o_ref):
    d = o_ref.shape[1]
    acc = None
    for i, r in enumerate((oa_ref, ob_ref, oc_ref)):
        t = jnp.dot(r[...], wb_ref[i], preferred_element_type=F32)
        term = sg_ref[:, i * d:(i + 1) * d].astype(F32) * t
        acc = term if acc is None else acc + term
    o_ref[...] = acc.astype(o_ref.dtype)


def _merge(oa, ob, oc, sg, wb, tm):
    rows, n = oa.shape
    d = wb.shape[2]
    row = lambda i: (i, 0)
    vmem = int(wb.size) * 2 + 2 * tm * (3 * n * 2 + N_BRANCH * d * 2 + d * 2) + 3 * tm * d * 4 + (8 << 20)
    return pl.pallas_call(
        _merge_kernel,
        grid=(rows // tm,),
        in_specs=[pl.BlockSpec((tm, n), row)] * 3
                 + [pl.BlockSpec((tm, N_BRANCH * d), row), _const_spec(wb.shape)],
        out_specs=pl.BlockSpec((tm, d), row),
        out_shape=jax.ShapeDtypeStruct((rows, d), BF16),
        compiler_params=_params(("parallel",), vmem),
        name="merge",
    )(oa, ob, oc, sg, wb)


def _out_kernel(m_ref, x_ref, w_ref, g_ref, o_ref):
    y = x_ref[...] + jnp.dot(m_ref[...], w_ref[...], preferred_element_type=F32)
    o_ref[...] = (_rms(y) * g_ref[...]).astype(o_ref.dtype)


def _outproj(merged, x, w, g, tm):
    rows, d = x.shape
    row = lambda i: (i, 0)
    vmem = d * d * 2 + 2 * tm * d * (2 + 4 + 4) + 3 * tm * d * 4 + (8 << 20)
    return pl.pallas_call(
        _out_kernel,
        grid=(rows // tm,),
        in_specs=[pl.BlockSpec((tm, d), row), pl.BlockSpec((tm, d), row),
                  _const_spec(w.shape), _const_spec(g.shape)],
        out_specs=pl.BlockSpec((tm, d), row),
        out_shape=jax.ShapeDtypeStruct((rows, d), x.dtype),
        compiler_params=_params(("parallel",), vmem),
        name="out_proj",
    )(merged, x, w, g)


def _wprep_kernel(w_ref, ws_ref, wz_ref, wg_ref):
    n_small = Q_LORA + KV_LORA + MLA_ROPE
    z0 = n_small + (_C_END - _C_QG)
    rows = w_ref.shape[0]
    ws_ref[:, :n_small] = w_ref[:, :n_small].astype(BF16)
    ws_ref[:, n_small:_C_QG] = jnp.zeros((rows, _C_QG - n_small), BF16)
    ws_ref[:, _C_QG:] = w_ref[:, n_small:z0].astype(BF16)
    wz_ref[...] = w_ref[:, z0:z0 + N_BRANCH * BRANCH_W].astype(BF16)
    wg_ref[...] = w_ref[:, z0 + N_BRANCH * BRANCH_W:].astype(BF16)


def _wprep(w_in, tr):
    d, n = w_in.shape
    widths = (_C_END, N_BRANCH * BRANCH_W, N_BRANCH * D_MODEL)
    return pl.pallas_call(
        _wprep_kernel,
        grid=(d // tr,),
        in_specs=[pl.BlockSpec((tr, n), lambda i: (i, 0))],
        out_specs=[pl.BlockSpec((tr, w), lambda i: (i, 0)) for w in widths],
        out_shape=[jax.ShapeDtypeStruct((d, w), BF16) for w in widths],
        compiler_params=_params(("parallel",), 2 * tr * n * 8 + 2 * tr * sum(widths) * 2 + (8 << 20)),
        name="weight_prep",
    )(w_in)


def _rope_tables(seq):
    rows = seq // GRID_W
    row = np.repeat(np.arange(rows, dtype=np.float32), GRID_W)
    col = np.tile(np.arange(GRID_W, dtype=np.float32), rows)

    def table(width):
        half = width // 4
        freqs = (ROPE_BASE ** (-np.arange(half, dtype=np.float32) / half)).astype(np.float32)
        parts_c, parts_s = [], []
        for pos in (row, col):
            ang = (pos[:, None] * freqs[None, :]).astype(np.float64)
            c, sn = np.cos(ang), np.sin(ang)
            parts_c += [c, c]
            parts_s += [-sn, sn]
        pad = ((0, 0), (0, LANES - width))
        c = np.pad(np.concatenate(parts_c, axis=-1), pad)
        sn = np.pad(np.concatenate(parts_s, axis=-1), pad)
        return jnp.asarray(c, F32), jnp.asarray(sn, F32)

    cm, sm = table(MLA_ROPE)
    cg, sg = table(GQA_HEAD_DIM)
    return cm, sm, cg, sg


def _prep_layer(w_in, w_q_b, w_kv_b, w_mem_kv, w_branch, w_out):
    w_small, w_z, w_gl = _wprep(w_in, LANES)
    wvgt = w_small[:, _C_VG:_C_QM].T
    wqb =w_q_b.reshape(Q_LORA, MLA_HEADS, MLA_NOPE + MLA_ROPE)
    wqb = jnp.pad(wqb, ((0, 0), (0, 0), (0, MLA_QK_PAD - MLA_NOPE - MLA_ROPE)))
    wqb = wqb.reshape(Q_LORA, MLA_HEADS * MLA_QK_PAD).astype(BF16)
    wkv = w_kv_b.reshape(KV_LORA, MLA_HEADS, MLA_NOPE + MLA_V)
    wkb = wkv[:, :, :MLA_NOPE].reshape(KV_LORA, -1).astype(BF16)
    wvbt = wkv[:, :, MLA_NOPE:].reshape(KV_LORA, -1).T.astype(BF16)
    return (w_small, wvgt, w_z, w_gl, wqb, wkb, wvbt, w_mem_kv.astype(BF16), w_branch.astype(BF16),
            w_out.astype(BF16))


def _tile(n, pref):
    t = min(n, pref)
    assert n % t == 0, (n, t)
    return t


def _layer(x, mem, tabs, batch, seq, norm_g, mla_q_norm_g, mla_kv_norm_g, gqa_q_norm_g, gqa_k_norm_g,
           mem_norm_g, final_g, prepped):
    w_small, wvgt, w_z, w_gl, wqb, wkb, wvbt, wmem, wbr, wout = prepped
    rows = batch * seq
    r2 = lambda g: g.reshape(1, -1)
    tm = _tile(seq, 512)

    h, qmla, kmla, vmla, qg, kg, vg, qm = _small(
        x, r2(norm_g), w_small, wvgt, r2(mla_q_norm_g), wqb, r2(mla_kv_norm_g), wkb, wvbt, r2(gqa_q_norm_g),
        r2(gqa_k_norm_g), tabs, seq, tm)
    tg = _tile(rows, 1024)
    sz = _gate(h, w_z, True, tg, 1536, "gate_z")
    sg = _gate(h, w_gl, False, tg, 1536, "gate_g")

    bq = _tile(seq // 2, 512)
    o_mla = _attn(qmla, kmla, vmla, sz, 0, batch, seq, MLA_HEADS, MLA_HEADS, MLA_QK_PAD, bq, "attn_mla")
    o_gqa = _attn(qg, kg, vg, sz, BRANCH_W // LANES, batch, seq, GQA_HEADS, GQA_KV_HEADS, GQA_HEAD_DIM, bq,
                  "attn_gqa")
    km, vm = _memkv(mem, r2(mem_norm_g), wmem)
    o_mem = _memattn(qm, km, vm, sz, batch, seq, _tile(seq, 1024))

    merged = _merge(o_mla, o_gqa, o_mem, sg, wbr, tm)
    return _outproj(merged, x, wout, r2(final_g), tm)


def _trunk(x, mem, tabs, gains, prepped):
    batch, seq, d = x.shape
    y = _layer(x.reshape(batch * seq, d), mem.reshape(batch * mem.shape[1], d), tabs, batch, seq, *gains,
               prepped)
    return y.reshape(batch, seq, d)


def kernel(x_prompt, x_sample, mem_prompt, mem_sample, norm_g, w_in, mla_q_norm_g, w_q_b, mla_kv_norm_g,
           w_kv_b, gqa_q_norm_g, gqa_k_norm_g, mem_norm_g, w_mem_kv, w_branch, w_out, final_norm_g):
    assert w_in.shape[0] == 1
    first = lambda a: a.reshape(a.shape[1:])
    prepped = _prep_layer(first(w_in), first(w_q_b), first(w_kv_b), first(w_mem_kv), first(w_branch),
                          first(w_out))
    gains = (first(norm_g), first(mla_q_norm_g), first(mla_kv_norm_g), first(gqa_q_norm_g),
             first(gqa_k_norm_g), first(mem_norm_g), final_norm_g)
    assert x_prompt.shape[1] == x_sample.shape[1]
    tabs = _rope_tables(x_prompt.shape[1])
    return (_trunk(x_prompt, mem_prompt, tabs, gains, prepped),
            _trunk(x_sample, mem_sample, tabs, gains, prepped))
```

```python
import functools
import math

import jax
import jax.numpy as jnp
import numpy as np
from jax import lax
from jax.experimental import pallas as pl
from jax.experimental.pallas import tpu as pltpu

F32 = jnp.float32
BF16 = jnp.bfloat16

D_MODEL = 2048
MLA_HEADS = 8
MLA_NOPE = 128
MLA_ROPE = 64
MLA_V = 128
Q_LORA = 512
KV_LORA = 256
GQA_HEADS = 8
GQA_KV_HEADS = 2
GQA_HEAD_DIM = 128
MEM_HEADS = 4
MEM_HEAD_DIM = 256
MEM_TOKENS = 256
BRANCH_W = 1024
N_BRANCH = 3
GRID_W = 64
ROPE_BASE = 10000.0
EPS = 1e-6
LOG2E = math.log2(math.e)
MLA_SCALE = LOG2E / math.sqrt(MLA_NOPE + MLA_ROPE)
GQA_SCALE = LOG2E / math.sqrt(GQA_HEAD_DIM)

LANES = 128
MLA_QK_PAD = 2 * LANES
VT_ROWS = LANES + 16
VMEM_CAP = 60000 * 1024
GATE_PARTS = 6

_C_CQ = 0
_C_CKV = _C_CQ + Q_LORA
_C_KPE = _C_CKV + KV_LORA
_C_QG = _C_KPE + LANES
_C_KG = _C_QG + GQA_HEADS * GQA_HEAD_DIM
_C_VG = _C_KG + GQA_KV_HEADS * GQA_HEAD_DIM
_C_QM = _C_VG + GQA_KV_HEADS * GQA_HEAD_DIM
_C_END = _C_QM + MEM_HEADS * MEM_HEAD_DIM


def _params(semantics, vmem_bytes):
    return pltpu.CompilerParams(dimension_semantics=semantics,
                                vmem_limit_bytes=min(int(vmem_bytes), VMEM_CAP))


def _const_spec(shape):
    nd = len(shape)
    return pl.BlockSpec(shape, lambda *_: (0,) * nd, pipeline_mode=pl.Buffered(1))


def _rms(x):
    return x * lax.rsqrt(jnp.mean(x * x, axis=-1, keepdims=True) + EPS)


def _rope(x, cos, sin_signed, first, half):
    n = x.shape[-1]
    partner = jnp.where(first, pltpu.roll(x, n - half, 1), pltpu.roll(x, half, 1))
    return x * cos + partner * sin_signed


def _small_kernel(x_ref, gx_ref, w_ref, wvgt_ref, gq_ref, wqb_ref, gkv_ref, wkb_ref, wvbt_ref, gqg_ref, gkg_ref,
                  cosm_ref, sinm_ref, cosg_ref, sing_ref,
                  h_ref, qmla_ref, kmla_ref, vmla_ref, qg_ref, kg_ref, vg_ref, qm_ref):
    tm = x_ref.shape[0]
    h = (_rms(x_ref[...]) * gx_ref[...]).astype(BF16)
    h_ref[...] = h

    acc = lax.dot_general(h, w_ref[...], (((1,), (1,)), ((), ())), preferred_element_type=F32)

    def proj(lo, hi):
        return acc[:, lo:hi]

    lane = lax.broadcasted_iota(jnp.int32, (tm, LANES), 1)
    first_m = (lane % (MLA_ROPE // 2)) < (MLA_ROPE // 4)
    first_g = (lane % (GQA_HEAD_DIM // 2)) < (GQA_HEAD_DIM // 4)
    cosm, sinm = cosm_ref[...], sinm_ref[...]
    cosg, sing = cosg_ref[...], sing_ref[...]
    ones = jnp.ones((VT_ROWS - LANES, tm), BF16)
    nt = (((1,), (1,)), ((), ()))

    cqn = (_rms(proj(_C_CQ, _C_CKV)) * gq_ref[...]).astype(BF16)
    q = jnp.dot(cqn, wqb_ref[...], preferred_element_type=F32)
    for hd in range(MLA_HEADS):
        c0 = hd * MLA_QK_PAD
        qmla_ref[:, c0:c0 + LANES] = (q[:, c0:c0 + LANES] * MLA_SCALE).astype(BF16)
        pe = _rope(q[:, c0 + LANES:c0 + 2 * LANES], cosm, sinm, first_m, MLA_ROPE // 4)
        qmla_ref[:, c0 + LANES:c0 + 2 * LANES] = (pe * MLA_SCALE).astype(BF16)

    ckvn = (_rms(proj(_C_CKV, _C_KPE)) * gkv_ref[...]).astype(BF16)
    kn = jnp.dot(ckvn, wkb_ref[...], preferred_element_type=F32)
    vt = lax.dot_general(wvbt_ref[...], ckvn, nt, preferred_element_type=F32)
    kpe = _rope(proj(_C_KPE, _C_QG), cosm, sinm, first_m, MLA_ROPE // 4).astype(BF16)
    for hd in range(MLA_HEADS):
        c0 = hd * MLA_QK_PAD
        kmla_ref[:, c0:c0 + LANES] = kn[:, hd * MLA_NOPE:(hd + 1) * MLA_NOPE].astype(BF16)
        kmla_ref[:, c0 + LANES:c0 + 2 * LANES] = kpe
        v0 = hd * VT_ROWS
        vmla_ref[v0:v0 + LANES, :] = vt[hd * MLA_V:(hd + 1) * MLA_V].astype(BF16)
        vmla_ref[v0 + LANES:v0 + VT_ROWS, :] = ones

    qg = proj(_C_QG, _C_KG)
    for hd in range(GQA_HEADS):
        sl = slice(hd * GQA_HEAD_DIM, (hd + 1) * GQA_HEAD_DIM)
        xn = _rms(qg[:, sl]) * gqg_ref[...]
        qg_ref[:, sl] = (_rope(xn, cosg, sing, first_g, GQA_HEAD_DIM // 4) * GQA_SCALE).astype(BF16)
    kg = proj(_C_KG, _C_VG)
    vgt = lax.dot_general(wvgt_ref[...], h, nt, preferred_element_type=F32)
    for hd in range(GQA_KV_HEADS):
        sl = slice(hd * GQA_HEAD_DIM, (hd + 1) * GQA_HEAD_DIM)
        xn = _rms(kg[:, sl]) * gkg_ref[...]
        kg_ref[:, sl] = _rope(xn, cosg, sing, first_g, GQA_HEAD_DIM // 4).astype(BF16)
        v0 = hd * VT_ROWS
        vg_ref[v0:v0 + LANES, :] = vgt[sl].astype(BF16)
        vg_ref[v0 + LANES:v0 + VT_ROWS, :] = ones

    qm_ref[...] = proj(_C_QM, _C_END).astype(BF16)


def _small(x, gx, w_small, wvgt, gq, wqb, gkv, wkb, wvbt, gqg, gkg, tabs, seq, tm):
    rows, d = x.shape
    nblk = seq // tm
    row = lambda i: (i, 0)
    col = lambda i: (0, i)
    tab = lambda i: (i % nblk, 0)
    outs = ((d, False), (MLA_HEADS * MLA_QK_PAD, False), (MLA_HEADS * MLA_QK_PAD, False),
            (MLA_HEADS * VT_ROWS, True), (GQA_HEADS * GQA_HEAD_DIM, False), (GQA_KV_HEADS * GQA_HEAD_DIM, False),
            (GQA_KV_HEADS * VT_ROWS, True), (MEM_HEADS * MEM_HEAD_DIM, False))
    weights = (gx, w_small, wvgt, gq, wqb, gkv, wkb, wvbt, gqg, gkg)
    wbytes = sum(int(w.size) * w.dtype.itemsize for w in weights)
    vmem = wbytes + 2 * tm * (d * 4 + 4 * LANES * 4 + sum(w for w, _ in outs) * 2) + tm * 4096 * 4 * 3 + (8 << 20)
    return pl.pallas_call(
        _small_kernel,
        grid=(rows // tm,),
        in_specs=[pl.BlockSpec((tm, d), row)] + [_const_spec(w.shape) for w in weights]
                 + [pl.BlockSpec((tm, LANES), tab)] * 4,
        out_specs=[pl.BlockSpec((w, tm), col) if t else pl.BlockSpec((tm, w), row) for w, t in outs],
        out_shape=[jax.ShapeDtypeStruct((w, rows) if t else (rows, w), BF16) for w, t in outs],
        compiler_params=_params(("parallel",), vmem),
        name="small_proj",
    )(x, *weights, *tabs)


def _gate_kernel(h_ref, w_ref, o_ref, *, silu):
    h = h_ref[...]
    step = w_ref.shape[0] // GATE_PARTS
    for p in range(GATE_PARTS):
        a = lax.dot_general(h, w_ref[p * step:(p + 1) * step, :], (((1,), (1,)), ((), ())),
                            preferred_element_type=F32)
        s = 0.5 + 0.5 * jnp.tanh(0.5 * a)
        o_ref[:, p * step:(p + 1) * step] = (a * s if silu else s).astype(o_ref.dtype)


def _gate(h, w, silu, tm, tn, name):
    rows, d = h.shape
    n = w.shape[0]
    assert tn % (GATE_PARTS * LANES) == 0
    vmem = 2 * (tm * d * 2 + d * tn * 2 + tm * tn * 2) + 3 * tm * tn * 4 + (8 << 20)
    return pl.pallas_call(
        functools.partial(_gate_kernel, silu=silu),
        grid=(n // tn, rows // tm),
        in_specs=[pl.BlockSpec((tm, d), lambda j, i: (i, 0)),
                  pl.BlockSpec((tn, d), lambda j, i: (j, 0))],
        out_specs=pl.BlockSpec((tm, tn), lambda j, i: (i, j)),
        out_shape=jax.ShapeDtypeStruct((rows, n), BF16),
        compiler_params=_params(("parallel", "parallel"), vmem),
        name=name,
    )(h, w)


def _attn_kernel(q_ref, k_ref, qn_ref, kn_ref, vt_ref, sz_ref, o_ref, s0, s1, m0, m1, *, bq):
    nq = q_ref.shape[0] // bq
    assert nq % 2 == 0 and nq >= 2

    def rows(t):
        return pl.ds(pl.multiple_of(t * bq, bq), bq)

    def scores(q, k_r, s_scr, m_scr):
        st = lax.dot_general(k_r[...], q, (((1,), (1,)), ((), ())), preferred_element_type=F32)
        s_scr[...] = st
        m_scr[...] = jnp.max(st, axis=0, keepdims=True)

    def finish(t, s_scr, m_scr):
        pt = jnp.exp2(s_scr[...] - m_scr[...]).astype(BF16)
        ot = jnp.dot(vt_ref[...], pt, preferred_element_type=F32)
        o = (ot[:LANES] / ot[LANES:LANES + 1]).T
        o_ref[rows(t), :] = (o * sz_ref[rows(t), :].astype(F32)).astype(o_ref.dtype)

    @pl.when((pl.program_id(0) == 0) & (pl.program_id(1) == 0))
    def _():
        scores(q_ref[rows(0), :], k_ref, s0, m0)

    def pair(j, carry):
        t = 2 * j
        scores(q_ref[rows(t + 1), :], k_ref, s1, m1)
        finish(t, s0, m0)
        scores(q_ref[rows(t + 2), :], k_ref, s0, m0)
        finish(t + 1, s1, m1)
        return carry

    lax.fori_loop(0, nq // 2 - 1, pair, 0)
    scores(q_ref[rows(nq - 1), :], k_ref, s1, m1)
    finish(nq - 2, s0, m0)
    scores(qn_ref[...], kn_ref, s0, m0)
    finish(nq - 1, s1, m1)


def _attn(q, k, vt, sz, sz_block0, batch, seq, heads, kv_heads, dk, bq, name):
    rows = batch * seq
    rep = heads // kv_heads
    nq = seq // bq

    def nxt(b, h):
        wrap = (h + 1) // heads
        return jnp.minimum(b + wrap, batch - 1), (h + 1) % heads

    def qn_map(b, h):
        nb, nh = nxt(b, h)
        return nb * nq, nh

    def kn_map(b, h):
        nb, nh = nxt(b, h)
        return nb, nh // rep

    vmem = 2 * (3 * seq * dk * 2 + bq * dk * 2 + VT_ROWS * seq * 2 + 2 * seq * LANES * 2) \
        + bq * seq * (2 * 4 + 4 + 2) + (8 << 20)
    return pl.pallas_call(
        functools.partial(_attn_kernel, bq=bq),
        grid=(batch, heads),
        in_specs=[pl.BlockSpec((seq, dk), lambda b, h: (b, h)),
                  pl.BlockSpec((seq, dk), lambda b, h: (b, h // rep)),
                  pl.BlockSpec((bq, dk), qn_map),
                  pl.BlockSpec((seq, dk), kn_map),
                  pl.BlockSpec((VT_ROWS, seq), lambda b, h: (h // rep, b)),
                  pl.BlockSpec((seq, LANES), lambda b, h: (b, sz_block0 + h))],
        out_specs=pl.BlockSpec((seq, LANES), lambda b, h: (b, h)),
        out_shape=jax.ShapeDtypeStruct((rows, heads * LANES), BF16),
        scratch_shapes=[pltpu.VMEM((seq, bq), F32), pltpu.VMEM((seq, bq), F32),
                        pltpu.VMEM((1, bq), F32), pltpu.VMEM((1, bq), F32)],
        compiler_params=_params(("arbitrary", "arbitrary"), vmem),
        name=name,
    )(q, k, q, k, vt, sz)


def _memkv_kernel(m_ref, g_ref, w_ref, k_ref, v_ref):
    mn = (_rms(m_ref[...]) * g_ref[...]).astype(BF16)
    kv = jnp.dot(mn, w_ref[...], preferred_element_type=F32)
    half = kv.shape[1] // 2
    k_ref[...] = kv[:, :half].astype(BF16)
    v_ref[...] = kv[:, half:].astype(BF16)


def _memkv(mem, g, w):
    rows, d = mem.shape
    n = w.shape[1] // 2
    blk = pl.BlockSpec((MEM_TOKENS, n), lambda b: (b, 0))
    vmem = d * 2 * n * 2 + 2 * MEM_TOKENS * (d * 4 + 2 * n * 2) + MEM_TOKENS * 2 * n * 8 + (8 << 20)
    return pl.pallas_call(
        _memkv_kernel,
        grid=(rows // MEM_TOKENS,),
        in_specs=[pl.BlockSpec((MEM_TOKENS, d), lambda b: (b, 0)),
                  _const_spec(g.shape), _const_spec(w.shape)],
        out_specs=[blk, blk],
        out_shape=[jax.ShapeDtypeStruct((rows, n), BF16)] * 2,
        compiler_params=_params(("parallel",), vmem),
        name="mem_kv",
    )(mem, g, w)


def _memattn_kernel(q_ref, k_ref, v_ref, sz_ref, o_ref):
    c = LOG2E / math.sqrt(MEM_HEAD_DIM)
    for hd in range(MEM_HEADS):
        sl = slice(hd * MEM_HEAD_DIM, (hd + 1) * MEM_HEAD_DIM)
        s = lax.dot_general(q_ref[:, sl], k_ref[:, sl], (((1,), (1,)), ((), ())),
                            preferred_element_type=F32)
        m = jnp.max(s, axis=-1, keepdims=True)
        e = jnp.exp2((s - m) * c)
        l = jnp.sum(e, axis=-1, keepdims=True)
        o = jnp.dot(e.astype(BF16), v_ref[:, sl], preferred_element_type=F32) / l
        o_ref[:, sl] = (o * sz_ref[:, sl].astype(F32)).astype(o_ref.dtype)


def _memattn(qm, km, vm, sz, batch, seq, tm):
    rows, n = qm.shape
    nq = seq // tm
    tile = lambda b, i: (b * nq + i, 0)
    vmem = 2 * (3 * tm * n * 2 + 2 * MEM_TOKENS * n * 2) + 6 * tm * n * 4 + (8 << 20)
    return pl.pallas_call(
        _memattn_kernel,
        grid=(batch, nq),
        in_specs=[pl.BlockSpec((tm, n), tile),
                  pl.BlockSpec((MEM_TOKENS, n), lambda b, i: (b, 0)),
                  pl.BlockSpec((MEM_TOKENS, n), lambda b, i: (b, 0)),
                  pl.BlockSpec((tm, n), lambda b, i: (b * nq + i, N_BRANCH - 1))],
        out_specs=pl.BlockSpec((tm, n), tile),
        out_shape=jax.ShapeDtypeStruct((rows, n), BF16),
        compiler_params=_params(("parallel", "parallel"), vmem),
        name="mem_attn",
    )(qm, km, vm, sz)


def _merge_kernel(oa_ref, ob_ref, oc_ref, sg_ref, wb_ref, o_ref):
    d = o_ref.shape[1]
    acc = None
    for i, r in enumerate((oa_ref, ob_ref, oc_ref)):
        t = jnp.dot(r[...], wb_ref[i], preferred_element_type=F32)
        term = sg_ref[:, i * d:(i + 1) * d].astype(F32) * t
        acc = term if acc is None else acc + term
    o_ref[...] = acc.astype(o_ref.dtype)


def _merge(oa, ob, oc, sg, wb, tm):
    rows, n = oa.shape
    d = wb.shape[2]
    row = lambda i: (i, 0)
    vmem = int(wb.size) * 2 + 2 * tm * (3 * n * 2 + N_BRANCH * d * 2 + d * 2) + 3 * tm * d * 4 + (8 << 20)
    return pl.pallas_call(
        _merge_kernel,
        grid=(rows // tm,),
        in_specs=[pl.BlockSpec((tm, n), row)] * 3
                 + [pl.BlockSpec((tm, N_BRANCH * d), row), _const_spec(wb.shape)],
        out_specs=pl.BlockSpec((tm, d), row),
        out_shape=jax.ShapeDtypeStruct((rows, d), BF16),
        compiler_params=_params(("parallel",), vmem),
        name="merge",
    )(oa, ob, oc, sg, wb)


def _out_kernel(m_ref, x_ref, w_ref, g_ref, o_ref):
    half = m_ref.shape[0] // 2
    for r in (slice(0, half), slice(half, 2 * half)):
        y = x_ref[r, :] + jnp.dot(m_ref[r, :], w_ref[...], preferred_element_type=F32)
        o_ref[r, :] = (_rms(y) * g_ref[...]).astype(o_ref.dtype)


def _outproj(merged, x, w, g, tm):
    rows, d = x.shape
    row = lambda i: (i, 0)
    vmem = d * d * 2 + 2 * tm * d * (2 + 4 + 4) + 3 * tm * d * 4 + (8 << 20)
    return pl.pallas_call(
        _out_kernel,
        grid=(rows // tm,),
        in_specs=[pl.BlockSpec((tm, d), row), pl.BlockSpec((tm, d), row),
                  _const_spec(w.shape), _const_spec(g.shape)],
        out_specs=pl.BlockSpec((tm, d), row),
        out_shape=jax.ShapeDtypeStruct((rows, d), x.dtype),
        compiler_params=_params(("parallel",), vmem),
        name="out_proj",
    )(merged, x, w, g)


def _wprep_kernel(wt_ref, ws_ref, wz_ref, wg_ref):
    n_small = Q_LORA + KV_LORA + MLA_ROPE
    z0 = n_small + (_C_END - _C_QG)
    cols = wt_ref.shape[1]
    ws_ref[:n_small, :] = wt_ref[:n_small, :].astype(BF16)
    ws_ref[n_small:_C_QG, :] = jnp.zeros((_C_QG - n_small, cols), BF16)
    ws_ref[_C_QG:, :] = wt_ref[n_small:z0, :].astype(BF16)
    wz_ref[...] = wt_ref[z0:z0 + N_BRANCH * BRANCH_W, :].astype(BF16)
    wg_ref[...] = wt_ref[z0 + N_BRANCH * BRANCH_W:, :].astype(BF16)


def _wprep(wt, tc):
    n, d = wt.shape
    heights = (_C_END, N_BRANCH * BRANCH_W, N_BRANCH * D_MODEL)
    return pl.pallas_call(
        _wprep_kernel,
        grid=(d // tc,),
        in_specs=[pl.BlockSpec((n, tc), lambda i: (0, i))],
        out_specs=[pl.BlockSpec((hh, tc), lambda i: (0, i)) for hh in heights],
        out_shape=[jax.ShapeDtypeStruct((hh, d), BF16) for hh in heights],
        compiler_params=_params(("parallel",), 2 * tc * n * 4 + 2 * tc * sum(heights) * 2 + (8 << 20)),
        name="weight_prep",
    )(wt)


def _rope_tables(seq):
    rows = seq // GRID_W
    row = np.repeat(np.arange(rows, dtype=np.float32), GRID_W)
    col = np.tile(np.arange(GRID_W, dtype=np.float32), rows)

    def table(width):
        half = width // 4
        freqs = (ROPE_BASE ** (-np.arange(half, dtype=np.float32) / half)).astype(np.float32)
        parts_c, parts_s = [], []
        for pos in (row, col):
            ang = (pos[:, None] * freqs[None, :]).astype(np.float64)
            c, sn = np.cos(ang), np.sin(ang)
            parts_c += [c, c]
            parts_s += [-sn, sn]
        pad = ((0, 0), (0, LANES - width))
        c = np.pad(np.concatenate(parts_c, axis=-1), pad)
        sn = np.pad(np.concatenate(parts_s, axis=-1), pad)
        return jnp.asarray(c, F32), jnp.asarray(sn, F32)

    cm, sm = table(MLA_ROPE)
    cg, sg = table(GQA_HEAD_DIM)
    return cm, sm, cg, sg


def _prep_layer(w_in, w_q_b, w_kv_b, w_mem_kv, w_branch, w_out):
    w_small, w_z, w_gl = _wprep(w_in.T, 2 * LANES)
    wvgt = w_small[_C_VG:_C_QM]
    wqb = w_q_b.reshape(Q_LORA, MLA_HEADS, MLA_NOPE + MLA_ROPE)
    wqb = jnp.pad(wqb, ((0, 0), (0, 0), (0, MLA_QK_PAD - MLA_NOPE - MLA_ROPE)))
    wqb = wqb.reshape(Q_LORA, MLA_HEADS * MLA_QK_PAD).astype(BF16)
    wkv = w_kv_b.reshape(KV_LORA, MLA_HEADS, MLA_NOPE + MLA_V)
    wkb = wkv[:, :, :MLA_NOPE].reshape(KV_LORA, -1).astype(BF16)
    wvbt = wkv[:, :, MLA_NOPE:].reshape(KV_LORA, -1).T.astype(BF16)
    return (w_small, wvgt, w_z, w_gl, wqb, wkb, wvbt, w_mem_kv.astype(BF16), w_branch.astype(BF16),
            w_out.astype(BF16))


def _tile(n, pref):
    t = min(n, pref)
    assert n % t == 0, (n, t)
    return t


def _layer(x, mem, tabs, batch, seq, norm_g, mla_q_norm_g, mla_kv_norm_g, gqa_q_norm_g, gqa_k_norm_g,
           mem_norm_g, final_g, prepped):
    w_small, wvgt, w_z, w_gl, wqb, wkb, wvbt, wmem, wbr, wout = prepped
    rows = batch * seq
    r2 = lambda g: g.reshape(1, -1)
    tm = _tile(seq, 512)

    h, qmla, kmla, vmla, qg, kg, vg, qm = _small(
        x, r2(norm_g), w_small, wvgt, r2(mla_q_norm_g), wqb, r2(mla_kv_norm_g), wkb, wvbt, r2(gqa_q_norm_g),
        r2(gqa_k_norm_g), tabs, seq, tm)
    tg = _tile(rows, 1024)
    sz = _gate(h, w_z, True, tg, 1536, "gate_z")
    sg = _gate(h, w_gl, False, tg, 1536, "gate_g")

    bq = _tile(seq // 2, 512)
    o_mla = _attn(qmla, kmla, vmla, sz, 0, batch, seq, MLA_HEADS, MLA_HEADS, MLA_QK_PAD, bq, "attn_mla")
    o_gqa = _attn(qg, kg, vg, sz, BRANCH_W // LANES, batch, seq, GQA_HEADS, GQA_KV_HEADS, GQA_HEAD_DIM, bq,
                  "attn_gqa")
    km, vm = _memkv(mem, r2(mem_norm_g), wmem)
    o_mem = _memattn(qm, km, vm, sz, batch, seq, _tile(seq, 1024))

    merged = _merge(o_mla, o_gqa, o_mem, sg, wbr, tm)
    return _outproj(merged, x, wout, r2(final_g), tm)


def _trunk(x, mem, tabs, gains, prepped):
    batch, seq, d = x.shape
    y = _layer(x.reshape(batch * seq, d), mem.reshape(batch * mem.shape[1], d), tabs, batch, seq, *gains,
               prepped)
    return y.reshape(batch, seq, d)


def kernel(x_prompt, x_sample, mem_prompt, mem_sample, norm_g, w_in, mla_q_norm_g, w_q_b, mla_kv_norm_g,
           w_kv_b, gqa_q_norm_g, gqa_k_norm_g, mem_norm_g, w_mem_kv, w_branch, w_out, final_norm_g):
    assert w_in.shape[0] == 1
    first = lambda a: a.reshape(a.shape[1:])
    prepped = _prep_layer(first(w_in), first(w_q_b), first(w_kv_b), first(w_mem_kv), first(w_branch),
                          first(w_out))
    gains = (first(norm_g), first(mla_q_norm_g), first(mla_kv_norm_g), first(gqa_q_norm_g),
             first(gqa_k_norm_g), first(mem_norm_g), final_norm_g)
    assert x_prompt.shape[1] == x_sample.shape[1]
    tabs = _rope_tables(x_prompt.shape[1])
    return (_trunk(x_prompt, mem_prompt, tabs, gains, prepped),
            _trunk(x_sample, mem_sample, tabs, gains, prepped))
```

```python
import functools
import math

import jax
import jax.numpy as jnp
import numpy as np
from jax import lax
from jax.experimental import pallas as pl
from jax.experimental.pallas import tpu as pltpu

F32 = jnp.float32
BF16 = jnp.bfloat16

D_MODEL = 2048
MLA_HEADS = 8
MLA_NOPE = 128
MLA_ROPE = 64
MLA_V = 128
Q_LORA = 512
KV_LORA = 256
GQA_HEADS = 8
GQA_KV_HEADS = 2
GQA_HEAD_DIM = 128
MEM_HEADS = 4
MEM_HEAD_DIM = 256
MEM_TOKENS = 256
BRANCH_W = 1024
N_BRANCH = 3
GRID_W = 64
ROPE_BASE = 10000.0
EPS = 1e-6
LOG2E = math.log2(math.e)
MLA_SCALE = LOG2E / math.sqrt(MLA_NOPE + MLA_ROPE)
GQA_SCALE = LOG2E / math.sqrt(GQA_HEAD_DIM)

LANES = 128
MLA_QK_PAD = 2 * LANES
VT_ROWS = LANES + 16
VMEM_CAP = 60000 * 1024
GATE_PARTS = 6

_C_CQ = 0
_C_CKV = _C_CQ + Q_LORA
_C_KPE = _C_CKV + KV_LORA
_C_QG = _C_KPE + LANES
_C_KG = _C_QG + GQA_HEADS * GQA_HEAD_DIM
_C_VG = _C_KG + GQA_KV_HEADS * GQA_HEAD_DIM
_C_QM = _C_VG + GQA_KV_HEADS * GQA_HEAD_DIM
_C_END = _C_QM + MEM_HEADS * MEM_HEAD_DIM


def _params(semantics, vmem_bytes):
    return pltpu.CompilerParams(dimension_semantics=semantics,
                                vmem_limit_bytes=min(int(vmem_bytes), VMEM_CAP))


def _const_spec(shape):
    nd = len(shape)
    return pl.BlockSpec(shape, lambda *_: (0,) * nd, pipeline_mode=pl.Buffered(1))


def _rms(x):
    return x * lax.rsqrt(jnp.mean(x * x, axis=-1, keepdims=True) + EPS)


def _rope(x, cos, sin_signed, first, half):
    n = x.shape[-1]
    partner = jnp.where(first, pltpu.roll(x, n - half, 1), pltpu.roll(x, half, 1))
    return x * cos + partner * sin_signed


def _small_kernel(x_ref, gx_ref, w_ref, wvgt_ref, gq_ref, wqb_ref, gkv_ref, wkb_ref, wvbt_ref, gqg_ref, gkg_ref,
                  cosm_ref, sinm_ref, cosg_ref, sing_ref,
                  h_ref, qmla_ref, kmla_ref, vmla_ref, qg_ref, kg_ref, vg_ref, qm_ref):
    tm = x_ref.shape[0]
    h = (_rms(x_ref[...]) * gx_ref[...]).astype(BF16)
    h_ref[...] = h

    acc = jnp.dot(h, w_ref[...], preferred_element_type=F32)

    def proj(lo, hi):
        return acc[:, lo:hi]

    lane = lax.broadcasted_iota(jnp.int32, (tm, LANES), 1)
    first_m = (lane % (MLA_ROPE // 2)) < (MLA_ROPE // 4)
    first_g = (lane % (GQA_HEAD_DIM // 2)) < (GQA_HEAD_DIM // 4)
    cosm, sinm = cosm_ref[...], sinm_ref[...]
    cosg, sing = cosg_ref[...], sing_ref[...]
    ones = jnp.ones((VT_ROWS - LANES, tm), BF16)
    nt = (((1,), (1,)), ((), ()))

    cqn = (_rms(proj(_C_CQ, _C_CKV)) * gq_ref[...]).astype(BF16)
    q = jnp.dot(cqn, wqb_ref[...], preferred_element_type=F32)
    for hd in range(MLA_HEADS):
        c0 = hd * MLA_QK_PAD
        qmla_ref[:, c0:c0 + LANES] = (q[:, c0:c0 + LANES] * MLA_SCALE).astype(BF16)
        pe = _rope(q[:, c0 + LANES:c0 + 2 * LANES], cosm, sinm, first_m, MLA_ROPE // 4)
        qmla_ref[:, c0 + LANES:c0 + 2 * LANES] = (pe * MLA_SCALE).astype(BF16)

    ckvn = (_rms(proj(_C_CKV, _C_KPE)) * gkv_ref[...]).astype(BF16)
    kn = jnp.dot(ckvn, wkb_ref[...], preferred_element_type=F32)
    vt = lax.dot_general(wvbt_ref[...], ckvn, nt, preferred_element_type=F32)
    kpe = _rope(proj(_C_KPE, _C_QG), cosm, sinm, first_m, MLA_ROPE // 4).astype(BF16)
    for hd in range(MLA_HEADS):
        c0 = hd * MLA_QK_PAD
        kmla_ref[:, c0:c0 + LANES] = kn[:, hd * MLA_NOPE:(hd + 1) * MLA_NOPE].astype(BF16)
        kmla_ref[:, c0 + LANES:c0 + 2 * LANES] = kpe
        v0 = hd * VT_ROWS
        vmla_ref[v0:v0 + LANES, :] = vt[hd * MLA_V:(hd + 1) * MLA_V].astype(BF16)
        vmla_ref[v0 + LANES:v0 + VT_ROWS, :] = ones

    qg = proj(_C_QG, _C_KG)
    for hd in range(GQA_HEADS):
        sl = slice(hd * GQA_HEAD_DIM, (hd + 1) * GQA_HEAD_DIM)
        xn = _rms(qg[:, sl]) * gqg_ref[...]
        qg_ref[:, sl] = (_rope(xn, cosg, sing, first_g, GQA_HEAD_DIM // 4) * GQA_SCALE).astype(BF16)
    kg = proj(_C_KG, _C_VG)
    vgt = lax.dot_general(wvgt_ref[...], h, nt, preferred_element_type=F32)
    for hd in range(GQA_KV_HEADS):
        sl = slice(hd * GQA_HEAD_DIM, (hd + 1) * GQA_HEAD_DIM)
        xn = _rms(kg[:, sl]) * gkg_ref[...]
        kg_ref[:, sl] = _rope(xn, cosg, sing, first_g, GQA_HEAD_DIM // 4).astype(BF16)
        v0 = hd * VT_ROWS
        vg_ref[v0:v0 + LANES, :] = vgt[sl].astype(BF16)
        vg_ref[v0 + LANES:v0 + VT_ROWS, :] = ones

    qm_ref[...] = proj(_C_QM, _C_END).astype(BF16)


def _small(x, gx, w_small, wvgt, gq, wqb, gkv, wkb, wvbt, gqg, gkg, tabs, seq, tm):
    rows, d = x.shape
    nblk = seq // tm
    row = lambda i: (i, 0)
    col = lambda i: (0, i)
    tab = lambda i: (i % nblk, 0)
    outs = ((d, False), (MLA_HEADS * MLA_QK_PAD, False), (MLA_HEADS * MLA_QK_PAD, False),
            (MLA_HEADS * VT_ROWS, True), (GQA_HEADS * GQA_HEAD_DIM, False), (GQA_KV_HEADS * GQA_HEAD_DIM, False),
            (GQA_KV_HEADS * VT_ROWS, True), (MEM_HEADS * MEM_HEAD_DIM, False))
    weights = (gx, w_small, wvgt, gq, wqb, gkv, wkb, wvbt, gqg, gkg)
    wbytes = sum(int(w.size) * w.dtype.itemsize for w in weights)
    vmem = wbytes + 2 * tm * (d * 4 + 4 * LANES * 4 + sum(w for w, _ in outs) * 2) + tm * 4096 * 4 * 3 + (8 << 20)
    return pl.pallas_call(
        _small_kernel,
        grid=(rows // tm,),
        in_specs=[pl.BlockSpec((tm, d), row)] + [_const_spec(w.shape) for w in weights]
                 + [pl.BlockSpec((tm, LANES), tab)] * 4,
        out_specs=[pl.BlockSpec((w, tm), col) if t else pl.BlockSpec((tm, w), row) for w, t in outs],
        out_shape=[jax.ShapeDtypeStruct((w, rows) if t else (rows, w), BF16) for w, t in outs],
        compiler_params=_params(("parallel",), vmem),
        name="small_proj",
    )(x, *weights, *tabs)


def _gate_kernel(h_ref, w_ref, o_ref, *, silu):
    h = h_ref[...]
    step = w_ref.shape[1] // GATE_PARTS
    for p in range(GATE_PARTS):
        a = jnp.dot(h, w_ref[:, p * step:(p + 1) * step], preferred_element_type=F32)
        s = 0.5 + 0.5 * jnp.tanh(0.5 * a)
        o_ref[:, p * step:(p + 1) * step] = (a * s if silu else s).astype(o_ref.dtype)


def _gate(h, w, silu, tm, tn, name):
    rows, d = h.shape
    n = w.shape[1]
    assert tn % (GATE_PARTS * LANES) == 0
    vmem = 2 * (tm * d * 2 + d * tn * 2 + tm * tn * 2) + 3 * tm * tn * 4 + (8 << 20)
    return pl.pallas_call(
        functools.partial(_gate_kernel, silu=silu),
        grid=(n // tn, rows // tm),
        in_specs=[pl.BlockSpec((tm, d), lambda j, i: (i, 0)),
                  pl.BlockSpec((d, tn), lambda j, i: (0, j))],
        out_specs=pl.BlockSpec((tm, tn), lambda j, i: (i, j)),
        out_shape=jax.ShapeDtypeStruct((rows, n), BF16),
        compiler_params=_params(("parallel", "parallel"), vmem),
        name=name,
    )(h, w)


def _attn_kernel(q_ref, k_ref, qn_ref, kn_ref, vt_ref, sz_ref, o_ref, s0, s1, m0, m1, *, bq):
    nq = q_ref.shape[0] // bq
    assert nq % 2 == 0 and nq >= 2

    def rows(t):
        return pl.ds(pl.multiple_of(t * bq, bq), bq)

    def scores(q, k_r, s_scr, m_scr):
        st = lax.dot_general(k_r[...], q, (((1,), (1,)), ((), ())), preferred_element_type=F32)
        s_scr[...] = st
        m_scr[...] = jnp.max(st, axis=0, keepdims=True)

    def finish(t, s_scr, m_scr):
        pt = jnp.exp2(s_scr[...] - m_scr[...]).astype(BF16)
        ot = jnp.dot(vt_ref[...], pt, preferred_element_type=F32)
        o = (ot[:LANES] / ot[LANES:LANES + 1]).T
        o_ref[rows(t), :] = (o * sz_ref[rows(t), :].astype(F32)).astype(o_ref.dtype)

    @pl.when((pl.program_id(0) == 0) & (pl.program_id(1) == 0))
    def _():
        scores(q_ref[rows(0), :], k_ref, s0, m0)

    def pair(j, carry):
        t = 2 * j
        scores(q_ref[rows(t + 1), :], k_ref, s1, m1)
        finish(t, s0, m0)
        scores(q_ref[rows(t + 2), :], k_ref, s0, m0)
        finish(t + 1, s1, m1)
        return carry

    lax.fori_loop(0, nq // 2 - 1, pair, 0)
    scores(q_ref[rows(nq - 1), :], k_ref, s1, m1)
    finish(nq - 2, s0, m0)
    scores(qn_ref[...], kn_ref, s0, m0)
    finish(nq - 1, s1, m1)


def _attn(q, k, vt, sz, sz_block0, batch, seq, heads, kv_heads, dk, bq, name):
    rows = batch * seq
    rep = heads // kv_heads
    nq = seq // bq

    def nxt(b, h):
        wrap = (h + 1) // heads
        return jnp.minimum(b + wrap, batch - 1), (h + 1) % heads

    def qn_map(b, h):
        nb, nh = nxt(b, h)
        return nb * nq, nh

    def kn_map(b, h):
        nb, nh = nxt(b, h)
        return nb, nh // rep

    vmem = 2 * (3 * seq * dk * 2 + bq * dk * 2 + VT_ROWS * seq * 2 + 2 * seq * LANES * 2) \
        + bq * seq * (2 * 4 + 4 + 2) + (8 << 20)
    return pl.pallas_call(
        functools.partial(_attn_kernel, bq=bq),
        grid=(batch, heads),
        in_specs=[pl.BlockSpec((seq, dk), lambda b, h: (b, h)),
                  pl.BlockSpec((seq, dk), lambda b, h: (b, h // rep)),
                  pl.BlockSpec((bq, dk), qn_map),
                  pl.BlockSpec((seq, dk), kn_map),
                  pl.BlockSpec((VT_ROWS, seq), lambda b, h: (h // rep, b)),
                  pl.BlockSpec((seq, LANES), lambda b, h: (b, sz_block0 + h))],
        out_specs=pl.BlockSpec((seq, LANES), lambda b, h: (b, h)),
        out_shape=jax.ShapeDtypeStruct((rows, heads * LANES), BF16),
        scratch_shapes=[pltpu.VMEM((seq, bq), F32), pltpu.VMEM((seq, bq), F32),
                        pltpu.VMEM((1, bq), F32), pltpu.VMEM((1, bq), F32)],
        compiler_params=_params(("arbitrary", "arbitrary"), vmem),
        name=name,
    )(q, k, q, k, vt, sz)


def _memkv_kernel(m_ref, g_ref, w_ref, k_ref, v_ref):
    mn = (_rms(m_ref[...]) * g_ref[...]).astype(BF16)
    kv = jnp.dot(mn, w_ref[...], preferred_element_type=F32)
    half = kv.shape[1] // 2
    k_ref[...] = kv[:, :half].astype(BF16)
    v_ref[...] = kv[:, half:].astype(BF16)


def _memkv(mem, g, w):
    rows, d = mem.shape
    n = w.shape[1] // 2
    blk = pl.BlockSpec((MEM_TOKENS, n), lambda b: (b, 0))
    vmem = d * 2 * n * 2 + 2 * MEM_TOKENS * (d * 4 + 2 * n * 2) + MEM_TOKENS * 2 * n * 8 + (8 << 20)
    return pl.pallas_call(
        _memkv_kernel,
        grid=(rows // MEM_TOKENS,),
        in_specs=[pl.BlockSpec((MEM_TOKENS, d), lambda b: (b, 0)),
                  _const_spec(g.shape), _const_spec(w.shape)],
        out_specs=[blk, blk],
        out_shape=[jax.ShapeDtypeStruct((rows, n), BF16)] * 2,
        compiler_params=_params(("parallel",), vmem),
        name="mem_kv",
    )(mem, g, w)


def _memattn_kernel(q_ref, k_ref, v_ref, sz_ref, o_ref):
    c = LOG2E / math.sqrt(MEM_HEAD_DIM)
    for hd in range(MEM_HEADS):
        sl = slice(hd * MEM_HEAD_DIM, (hd + 1) * MEM_HEAD_DIM)
        s = lax.dot_general(q_ref[:, sl], k_ref[:, sl], (((1,), (1,)), ((), ())),
                            preferred_element_type=F32)
        m = jnp.max(s, axis=-1, keepdims=True)
        e = jnp.exp2((s - m) * c)
        l = jnp.sum(e, axis=-1, keepdims=True)
        o = jnp.dot(e.astype(BF16), v_ref[:, sl], preferred_element_type=F32) / l
        o_ref[:, sl] = (o * sz_ref[:, sl].astype(F32)).astype(o_ref.dtype)


def _memattn(qm, km, vm, sz, batch, seq, tm):
    rows, n = qm.shape
    nq = seq // tm
    tile = lambda b, i: (b * nq + i, 0)
    vmem = 2 * (3 * tm * n * 2 + 2 * MEM_TOKENS * n * 2) + 6 * tm * n * 4 + (8 << 20)
    return pl.pallas_call(
        _memattn_kernel,
        grid=(batch, nq),
        in_specs=[pl.BlockSpec((tm, n), tile),
                  pl.BlockSpec((MEM_TOKENS, n), lambda b, i: (b, 0)),
                  pl.BlockSpec((MEM_TOKENS, n), lambda b, i: (b, 0)),
                  pl.BlockSpec((tm, n), lambda b, i: (b * nq + i, N_BRANCH - 1))],
        out_specs=pl.BlockSpec((tm, n), tile),
        out_shape=jax.ShapeDtypeStruct((rows, n), BF16),
        compiler_params=_params(("parallel", "parallel"), vmem),
        name="mem_attn",
    )(qm, km, vm, sz)


def _merge_kernel(oa_ref, ob_ref, oc_ref, sg_ref, wb_ref, o_ref):
    d = o_ref.shape[1]
    acc = None
    for i, r in enumerate((oa_ref, ob_ref, oc_ref)):
        t = jnp.dot(r[...], wb_ref[i], preferred_element_type=F32)
        term = sg_ref[:, i * d:(i + 1) * d].astype(F32) * t
        acc = term if acc is None else acc + term
    o_ref[...] = acc.astype(o_ref.dtype)


def _merge(oa, ob, oc, sg, wb, tm):
    rows, n = oa.shape
    d = wb.shape[2]
    row = lambda i: (i, 0)
    vmem = int(wb.size) * 2 + 2 * tm * (3 * n * 2 + N_BRANCH * d * 2 + d * 2) + 3 * tm * d * 4 + (8 << 20)
    return pl.pallas_call(
        _merge_kernel,
        grid=(rows // tm,),
        in_specs=[pl.BlockSpec((tm, n), row)] * 3
                 + [pl.BlockSpec((tm, N_BRANCH * d), row), _const_spec(wb.shape)],
        out_specs=pl.BlockSpec((tm, d), row),
        out_shape=jax.ShapeDtypeStruct((rows, d), BF16),
        compiler_params=_params(("parallel",), vmem),
        name="merge",
    )(oa, ob, oc, sg, wb)


def _out_kernel(m_ref, x_ref, w_ref, g_ref, o_ref):
    y = x_ref[...] + jnp.dot(m_ref[...], w_ref[...], preferred_element_type=F32)
    o_ref[...] = (_rms(y) * g_ref[...]).astype(o_ref.dtype)


def _outproj(merged, x, w, g, tm):
    rows, d = x.shape
    row = lambda i: (i, 0)
    vmem = d * d * 2 + 2 * tm * d * (2 + 4 + 4) + 3 * tm * d * 4 + (8 << 20)
    return pl.pallas_call(
        _out_kernel,
        grid=(rows // tm,),
        in_specs=[pl.BlockSpec((tm, d), row), pl.BlockSpec((tm, d), row),
                  _const_spec(w.shape), _const_spec(g.shape)],
        out_specs=pl.BlockSpec((tm, d), row),
        out_shape=jax.ShapeDtypeStruct((rows, d), x.dtype),
        compiler_params=_params(("parallel",), vmem),
        name="out_proj",
    )(merged, x, w, g)


def _wprep_kernel(wt_ref, ws_ref, wz_ref, wg_ref):
    n_small = Q_LORA + KV_LORA + MLA_ROPE
    z0 = n_small + (_C_END - _C_QG)
    cols = wt_ref.shape[1]
    ws_ref[:, :_C_QG] = jnp.concatenate(
        [wt_ref[:n_small, :], jnp.zeros((_C_QG - n_small, cols), F32)], axis=0).T.astype(BF16)
    ws_ref[:, _C_QG:] = wt_ref[n_small:z0, :].T.astype(BF16)
    wz_ref[...] = wt_ref[z0:z0 + N_BRANCH * BRANCH_W, :].T.astype(BF16)
    wg_ref[...] = wt_ref[z0 + N_BRANCH * BRANCH_W:, :].T.astype(BF16)


def _wprep(wt, tc):
    n, d = wt.shape
    widths = (_C_END, N_BRANCH * BRANCH_W, N_BRANCH * D_MODEL)
    return pl.pallas_call(
        _wprep_kernel,
        grid=(d // tc,),
        in_specs=[pl.BlockSpec((n, tc), lambda i: (0, i))],
        out_specs=[pl.BlockSpec((tc, w), lambda i: (i, 0)) for w in widths],
        out_shape=[jax.ShapeDtypeStruct((d, w), BF16) for w in widths],
        compiler_params=_params(("parallel",), 2 * tc * n * 4 + 2 * tc * sum(widths) * 2 + tc * n * 8 + (8 << 20)),
        name="weight_prep",
    )(wt)


def _rope_tables(seq):
    rows = seq // GRID_W
    row = np.repeat(np.arange(rows, dtype=np.float32), GRID_W)
    col = np.tile(np.arange(GRID_W, dtype=np.float32), rows)

    def table(width):
        half = width // 4
        freqs = (ROPE_BASE ** (-np.arange(half, dtype=np.float32) / half)).astype(np.float32)
        parts_c, parts_s = [], []
        for pos in (row, col):
            ang = (pos[:, None] * freqs[None, :]).astype(np.float64)
            c, sn = np.cos(ang), np.sin(ang)
            parts_c += [c, c]
            parts_s += [-sn, sn]
        pad = ((0, 0), (0, LANES - width))
        c = np.pad(np.concatenate(parts_c, axis=-1), pad)
        sn = np.pad(np.concatenate(parts_s, axis=-1), pad)
        return jnp.asarray(c, F32), jnp.asarray(sn, F32)

    cm, sm = table(MLA_ROPE)
    cg, sg = table(GQA_HEAD_DIM)
    return cm, sm, cg, sg


def _prep_layer(w_in, w_q_b, w_kv_b, w_mem_kv, w_branch, w_out):
    w_small, w_z, w_gl = _wprep(w_in.T, 2 * LANES)
    wvgt = w_small[:, _C_VG:_C_QM].T
    wqb = w_q_b.reshape(Q_LORA, MLA_HEADS, MLA_NOPE + MLA_ROPE)
    wqb = jnp.pad(wqb, ((0, 0), (0, 0), (0, MLA_QK_PAD - MLA_NOPE - MLA_ROPE)))
    wqb = wqb.reshape(Q_LORA, MLA_HEADS * MLA_QK_PAD).astype(BF16)
    wkv = w_kv_b.reshape(KV_LORA, MLA_HEADS, MLA_NOPE + MLA_V)
    wkb = wkv[:, :, :MLA_NOPE].reshape(KV_LORA, -1).astype(BF16)
    wvbt = wkv[:, :, MLA_NOPE:].reshape(KV_LORA, -1).T.astype(BF16)
    return (w_small, wvgt, w_z, w_gl, wqb, wkb, wvbt, w_mem_kv.astype(BF16), w_branch.astype(BF16),
            w_out.astype(BF16))


def _tile(n, pref):
    t = min(n, pref)
    assert n % t == 0, (n, t)
    return t


def _layer(x, mem, tabs, batch, seq, norm_g, mla_q_norm_g, mla_kv_norm_g, gqa_q_norm_g, gqa_k_norm_g,
           mem_norm_g, final_g, prepped):
    w_small, wvgt, w_z, w_gl, wqb, wkb, wvbt, wmem, wbr, wout = prepped
    rows = batch * seq
    r2 = lambda g: g.reshape(1, -1)
    tm = _tile(seq, 512)

    h, qmla, kmla, vmla, qg, kg, vg, qm = _small(
        x, r2(norm_g), w_small, wvgt, r2(mla_q_norm_g), wqb, r2(mla_kv_norm_g), wkb, wvbt, r2(gqa_q_norm_g),
        r2(gqa_k_norm_g), tabs, seq, tm)
    tg = _tile(rows, 1024)
    sz = _gate(h, w_z, True, tg, 1536, "gate_z")
    sg = _gate(h, w_gl, False, tg, 1536, "gate_g")

    bq = _tile(seq // 2, 512)
    o_mla = _attn(qmla, kmla, vmla, sz, 0, batch, seq, MLA_HEADS, MLA_HEADS, MLA_QK_PAD, bq, "attn_mla")
    o_gqa = _attn(qg, kg, vg, sz, BRANCH_W // LANES, batch, seq, GQA_HEADS, GQA_KV_HEADS, GQA_HEAD_DIM, bq,
                  "attn_gqa")
    km, vm = _memkv(mem, r2(mem_norm_g), wmem)
    o_mem = _memattn(qm, km, vm, sz, batch, seq, _tile(seq, 1024))

    merged = _merge(o_mla, o_gqa, o_mem, sg, wbr, tm)
    return _outproj(merged, x, wout, r2(final_g), tm)


def _trunk(x, mem, tabs, gains, prepped):
    batch, seq, d = x.shape
    y = _layer(x.reshape(batch * seq, d), mem.reshape(batch * mem.shape[1], d), tabs, batch, seq, *gains,
               prepped)
    return y.reshape(batch, seq, d)


def kernel(x_prompt, x_sample, mem_prompt, mem_sample, norm_g, w_in, mla_q_norm_g, w_q_b, mla_kv_norm_g,
           w_kv_b, gqa_q_norm_g, gqa_k_norm_g, mem_norm_g, w_mem_kv, w_branch, w_out, final_norm_g):
    assert w_in.shape[0] == 1
    first = lambda a: a.reshape(a.shape[1:])
    prepped = _prep_layer(first(w_in), first(w_q_b), first(w_kv_b), first(w_mem_kv), first(w_branch),
                          first(w_out))
    gains = (first(norm_g), first(mla_q_norm_g), first(mla_kv_norm_g), first(gqa_q_norm_g),
             first(gqa_k_norm_g), first(mem_norm_g), final_norm_g)
    assert x_prompt.shape[1] == x_sample.shape[1]
    tabs = _rope_tables(x_prompt.shape[1])
    return (_trunk(x_prompt, mem_prompt, tabs, gains, prepped),
            _trunk(x_sample, mem_sample, tabs, gains, prepped))
```

```python
import functools
import math

import jax
import jax.numpy as jnp
import numpy as np
from jax import lax
from jax.experimental import pallas as pl
from jax.experimental.pallas import tpu as pltpu

F32 = jnp.float32
BF16 = jnp.bfloat16

D_MODEL = 2048
MLA_HEADS = 8
MLA_NOPE = 128
MLA_ROPE = 64
MLA_V = 128
Q_LORA = 512
KV_LORA = 256
GQA_HEADS = 8
GQA_KV_HEADS = 2
GQA_HEAD_DIM = 128
MEM_HEADS = 4
MEM_HEAD_DIM = 256
MEM_TOKENS = 256
BRANCH_W = 1024
N_BRANCH = 3
GRID_W = 64
ROPE_BASE = 10000.0
EPS = 1e-6
LOG2E = math.log2(math.e)
MLA_SCALE = LOG2E / math.sqrt(MLA_NOPE + MLA_ROPE)
GQA_SCALE = LOG2E / math.sqrt(GQA_HEAD_DIM)

LANES = 128
MLA_QK_PAD = 2 * LANES
VT_ROWS = LANES + 16
VMEM_CAP = 60000 * 1024
GATE_TN = 3072
GATE_PARTS = 12

_C_CQ = 0
_C_CKV = _C_CQ + Q_LORA
_C_KPE = _C_CKV + KV_LORA
_C_QG = _C_KPE + LANES
_C_KG = _C_QG + GQA_HEADS * GQA_HEAD_DIM
_C_VG = _C_KG + GQA_KV_HEADS * GQA_HEAD_DIM
_C_QM = _C_VG + GQA_KV_HEADS * GQA_HEAD_DIM
_C_END = _C_QM + MEM_HEADS * MEM_HEAD_DIM


def _params(semantics, vmem_bytes):
    return pltpu.CompilerParams(dimension_semantics=semantics,
                                vmem_limit_bytes=min(int(vmem_bytes), VMEM_CAP))


def _const_spec(shape):
    nd = len(shape)
    return pl.BlockSpec(shape, lambda *_: (0,) * nd, pipeline_mode=pl.Buffered(1))


def _rms(x):
    return x * lax.rsqrt(jnp.mean(x * x, axis=-1, keepdims=True) + EPS)


def _rope(x, cos, sin_signed, first, half):
    n = x.shape[-1]
    partner = jnp.where(first, pltpu.roll(x, n - half, 1), pltpu.roll(x, half, 1))
    return x * cos + partner * sin_signed


def _small_kernel(x_ref, gx_ref, w_ref, wvgt_ref, gq_ref, wqb_ref, gkv_ref, wkb_ref, wvbt_ref, gqg_ref, gkg_ref,
                  cosm_ref, sinm_ref, cosg_ref, sing_ref,
                  h_ref, qmla_ref, kmla_ref, vmla_ref, qg_ref, kg_ref, vg_ref, qm_ref):
    tm = x_ref.shape[0]
    h = (_rms(x_ref[...]) * gx_ref[...]).astype(BF16)
    h_ref[...] = h

    acc = jnp.dot(h, w_ref[...], preferred_element_type=F32)

    def proj(lo, hi):
        return acc[:, lo:hi]

    lane = lax.broadcasted_iota(jnp.int32, (tm, LANES), 1)
    first_m = (lane % (MLA_ROPE // 2)) < (MLA_ROPE // 4)
    first_g = (lane % (GQA_HEAD_DIM // 2)) < (GQA_HEAD_DIM // 4)
    cosm, sinm = cosm_ref[...], sinm_ref[...]
    cosg, sing = cosg_ref[...], sing_ref[...]
    ones = jnp.ones((VT_ROWS - LANES, tm), BF16)
    nt = (((1,), (1,)), ((), ()))

    cqn = (_rms(proj(_C_CQ, _C_CKV)) * gq_ref[...]).astype(BF16)
    q = jnp.dot(cqn, wqb_ref[...], preferred_element_type=F32)
    for hd in range(MLA_HEADS):
        c0 = hd * MLA_QK_PAD
        qmla_ref[:, c0:c0 + LANES] = (q[:, c0:c0 + LANES] * MLA_SCALE).astype(BF16)
        pe = _rope(q[:, c0 + LANES:c0 + 2 * LANES], cosm, sinm, first_m, MLA_ROPE // 4)
        qmla_ref[:, c0 + LANES:c0 + 2 * LANES] = (pe * MLA_SCALE).astype(BF16)

    ckvn = (_rms(proj(_C_CKV, _C_KPE)) * gkv_ref[...]).astype(BF16)
    kn = jnp.dot(ckvn, wkb_ref[...], preferred_element_type=F32)
    vt = lax.dot_general(wvbt_ref[...], ckvn, nt, preferred_element_type=F32)
    kpe = _rope(proj(_C_KPE, _C_QG), cosm, sinm, first_m, MLA_ROPE // 4).astype(BF16)
    for hd in range(MLA_HEADS):
        c0 = hd * MLA_QK_PAD
        kmla_ref[:, c0:c0 + LANES] = kn[:, hd * MLA_NOPE:(hd + 1) * MLA_NOPE].astype(BF16)
        kmla_ref[:, c0 + LANES:c0 + 2 * LANES] = kpe
        v0 = hd * VT_ROWS
        vmla_ref[v0:v0 + LANES, :] = vt[hd * MLA_V:(hd + 1) * MLA_V].astype(BF16)
        vmla_ref[v0 + LANES:v0 + VT_ROWS, :] = ones

    qg = proj(_C_QG, _C_KG)
    for hd in range(GQA_HEADS):
        sl = slice(hd * GQA_HEAD_DIM, (hd + 1) * GQA_HEAD_DIM)
        xn = _rms(qg[:, sl]) * gqg_ref[...]
        qg_ref[:, sl] = (_rope(xn, cosg, sing, first_g, GQA_HEAD_DIM // 4) * GQA_SCALE).astype(BF16)
    kg = proj(_C_KG, _C_VG)
    vgt = lax.dot_general(wvgt_ref[...], h, nt, preferred_element_type=F32)
    for hd in range(GQA_KV_HEADS):
        sl = slice(hd * GQA_HEAD_DIM, (hd + 1) * GQA_HEAD_DIM)
        xn = _rms(kg[:, sl]) * gkg_ref[...]
        kg_ref[:, sl] = _rope(xn, cosg, sing, first_g, GQA_HEAD_DIM // 4).astype(BF16)
        v0 = hd * VT_ROWS
        vg_ref[v0:v0 + LANES, :] = vgt[sl].astype(BF16)
        vg_ref[v0 + LANES:v0 + VT_ROWS, :] = ones

    qm_ref[...] = proj(_C_QM, _C_END).astype(BF16)


def _small(x, gx, w_small, wvgt, gq, wqb, gkv, wkb, wvbt, gqg, gkg, tabs, seq, tm):
    rows, d = x.shape
    nblk = seq // tm
    row = lambda i: (i, 0)
    col = lambda i: (0, i)
    tab = lambda i: (i % nblk, 0)
    outs = ((d, False), (MLA_HEADS * MLA_QK_PAD, False), (MLA_HEADS * MLA_QK_PAD, False),
            (MLA_HEADS * VT_ROWS, True), (GQA_HEADS * GQA_HEAD_DIM, False), (GQA_KV_HEADS * GQA_HEAD_DIM, False),
            (GQA_KV_HEADS * VT_ROWS, True), (MEM_HEADS * MEM_HEAD_DIM, False))
    weights = (gx, w_small, wvgt, gq, wqb, gkv, wkb, wvbt, gqg, gkg)
    wbytes = sum(int(w.size) * w.dtype.itemsize for w in weights)
    vmem = wbytes + 2 * tm * (d * 4 + 4 * LANES * 4 + sum(w for w, _ in outs) * 2) + tm * 4096 * 4 * 3 + (8 << 20)
    return pl.pallas_call(
        _small_kernel,
        grid=(rows // tm,),
        in_specs=[pl.BlockSpec((tm, d), row)] + [_const_spec(w.shape) for w in weights]
                 + [pl.BlockSpec((tm, LANES), tab)] * 4,
        out_specs=[pl.BlockSpec((w, tm), col) if t else pl.BlockSpec((tm, w), row) for w, t in outs],
        out_shape=[jax.ShapeDtypeStruct((w, rows) if t else (rows, w), BF16) for w, t in outs],
        compiler_params=_params(("parallel",), vmem),
        name="small_proj",
    )(x, *weights, *tabs)


def _gate_kernel(h_ref, w_ref, o_ref, *, silu):
    h = h_ref[...]
    step = w_ref.shape[1] // GATE_PARTS
    for p in range(GATE_PARTS):
        a = jnp.dot(h, w_ref[:, p * step:(p + 1) * step], preferred_element_type=F32)
        s = 0.5 + 0.5 * jnp.tanh(0.5 * a)
        o_ref[:, p * step:(p + 1) * step] = (a * s if silu else s).astype(o_ref.dtype)


def _gate(h, w, silu, tm, tn, name):
    rows, d = h.shape
    n = w.shape[1]
    assert tn % (GATE_PARTS * LANES) == 0
    vmem = 2 * (tm * d * 2 + d * tn * 2 + tm * tn * 2) + 3 * tm * tn * 4 + (8 << 20)
    return pl.pallas_call(
        functools.partial(_gate_kernel, silu=silu),
        grid=(n // tn, rows // tm),
        in_specs=[pl.BlockSpec((tm, d), lambda j, i: (i, 0)),
                  pl.BlockSpec((d, tn), lambda j, i: (0, j))],
        out_specs=pl.BlockSpec((tm, tn), lambda j, i: (i, j)),
        out_shape=jax.ShapeDtypeStruct((rows, n), BF16),
        compiler_params=_params(("parallel", "parallel"), vmem),
        name=name,
    )(h, w)


def _attn_kernel(q_ref, k_ref, qn_ref, kn_ref, vt_ref, sz_ref, o_ref, s0, s1, m0, m1, *, bq):
    nq = q_ref.shape[0] // bq
    assert nq % 2 == 0 and nq >= 2

    def rows(t):
        return pl.ds(pl.multiple_of(t * bq, bq), bq)

    def scores(q, k_r, s_scr, m_scr):
        st = lax.dot_general(k_r[...], q, (((1,), (1,)), ((), ())), preferred_element_type=F32)
        s_scr[...] = st
        m_scr[...] = jnp.max(st, axis=0, keepdims=True)

    def finish(t, s_scr, m_scr):
        pt = jnp.exp2(s_scr[...] - m_scr[...]).astype(BF16)
        ot = jnp.dot(vt_ref[...], pt, preferred_element_type=F32)
        o = (ot[:LANES] / ot[LANES:LANES + 1]).T
        o_ref[rows(t), :] = (o * sz_ref[rows(t), :].astype(F32)).astype(o_ref.dtype)

    @pl.when((pl.program_id(0) == 0) & (pl.program_id(1) == 0))
    def _():
        scores(q_ref[rows(0), :], k_ref, s0, m0)

    def pair(j, carry):
        t = 2 * j
        scores(q_ref[rows(t + 1), :], k_ref, s1, m1)
        finish(t, s0, m0)
        scores(q_ref[rows(t + 2), :], k_ref, s0, m0)
        finish(t + 1, s1, m1)
        return carry

    lax.fori_loop(0, nq // 2 - 1, pair, 0)
    scores(q_ref[rows(nq - 1), :], k_ref, s1, m1)
    finish(nq - 2, s0, m0)
    scores(qn_ref[...], kn_ref, s0, m0)
    finish(nq - 1, s1, m1)


def _attn(q, k, vt, sz, sz_block0, batch, seq, heads, kv_heads, dk, bq, name):
    rows = batch * seq
    rep = heads // kv_heads
    nq = seq // bq

    def nxt(b, h):
        wrap = (h + 1) // heads
        return jnp.minimum(b + wrap, batch - 1), (h + 1) % heads

    def qn_map(b, h):
        nb, nh = nxt(b, h)
        return nb * nq, nh

    def kn_map(b, h):
        nb, nh = nxt(b, h)
        return nb, nh // rep

    vmem = 2 * (3 * seq * dk * 2 + bq * dk * 2 + VT_ROWS * seq * 2 + 2 * seq * LANES * 2) \
        + bq * seq * (2 * 4 + 4 + 2) + (8 << 20)
    return pl.pallas_call(
        functools.partial(_attn_kernel, bq=bq),
        grid=(batch, heads),
        in_specs=[pl.BlockSpec((seq, dk), lambda b, h: (b, h)),
                  pl.BlockSpec((seq, dk), lambda b, h: (b, h // rep)),
                  pl.BlockSpec((bq, dk), qn_map),
                  pl.BlockSpec((seq, dk), kn_map),
                  pl.BlockSpec((VT_ROWS, seq), lambda b, h: (h // rep, b)),
                  pl.BlockSpec((seq, LANES), lambda b, h: (b, sz_block0 + h))],
        out_specs=pl.BlockSpec((seq, LANES), lambda b, h: (b, h)),
        out_shape=jax.ShapeDtypeStruct((rows, heads * LANES), BF16),
        scratch_shapes=[pltpu.VMEM((seq, bq), F32), pltpu.VMEM((seq, bq), F32),
                        pltpu.VMEM((1, bq), F32), pltpu.VMEM((1, bq), F32)],
        compiler_params=_params(("arbitrary", "arbitrary"), vmem),
        name=name,
    )(q, k, q, k, vt, sz)


def _memkv_kernel(m_ref, g_ref, w_ref, k_ref, v_ref):
    mn = (_rms(m_ref[...]) * g_ref[...]).astype(BF16)
    kv = jnp.dot(mn, w_ref[...], preferred_element_type=F32)
    half = kv.shape[1] // 2
    k_ref[...] = kv[:, :half].astype(BF16)
    v_ref[...] = kv[:, half:].astype(BF16)


def _memkv(mem, g, w):
    rows, d = mem.shape
    n = w.shape[1] // 2
    blk = pl.BlockSpec((MEM_TOKENS, n), lambda b: (b, 0))
    vmem = d * 2 * n * 2 + 2 * MEM_TOKENS * (d * 4 + 2 * n * 2) + MEM_TOKENS * 2 * n * 8 + (8 << 20)
    return pl.pallas_call(
        _memkv_kernel,
        grid=(rows // MEM_TOKENS,),
        in_specs=[pl.BlockSpec((MEM_TOKENS, d), lambda b: (b, 0)),
                  _const_spec(g.shape), _const_spec(w.shape)],
        out_specs=[blk, blk],
        out_shape=[jax.ShapeDtypeStruct((rows, n), BF16)] * 2,
        compiler_params=_params(("parallel",), vmem),
        name="mem_kv",
    )(mem, g, w)


def _memattn_kernel(q_ref, k_ref, v_ref, sz_ref, o_ref):
    c = LOG2E / math.sqrt(MEM_HEAD_DIM)
    for hd in range(MEM_HEADS):
        sl = slice(hd * MEM_HEAD_DIM, (hd + 1) * MEM_HEAD_DIM)
        s = lax.dot_general(q_ref[:, sl], k_ref[:, sl], (((1,), (1,)), ((), ())),
                            preferred_element_type=F32)
        m = jnp.max(s, axis=-1, keepdims=True)
        e = jnp.exp2((s - m) * c)
        l = jnp.sum(e, axis=-1, keepdims=True)
        o = jnp.dot(e.astype(BF16), v_ref[:, sl], preferred_element_type=F32) / l
        o_ref[:, sl] = (o * sz_ref[:, sl].astype(F32)).astype(o_ref.dtype)


def _memattn(qm, km, vm, sz, batch, seq, tm):
    rows, n = qm.shape
    nq = seq // tm
    tile = lambda b, i: (b * nq + i, 0)
    vmem = 2 * (3 * tm * n * 2 + 2 * MEM_TOKENS * n * 2) + 6 * tm * n * 4 + (8 << 20)
    return pl.pallas_call(
        _memattn_kernel,
        grid=(batch, nq),
        in_specs=[pl.BlockSpec((tm, n), tile),
                  pl.BlockSpec((MEM_TOKENS, n), lambda b, i: (b, 0)),
                  pl.BlockSpec((MEM_TOKENS, n), lambda b, i: (b, 0)),
                  pl.BlockSpec((tm, n), lambda b, i: (b * nq + i, N_BRANCH - 1))],
        out_specs=pl.BlockSpec((tm, n), tile),
        out_shape=jax.ShapeDtypeStruct((rows, n), BF16),
        compiler_params=_params(("parallel", "parallel"), vmem),
        name="mem_attn",
    )(qm, km, vm, sz)


def _merge_kernel(oa_ref, ob_ref, oc_ref, sg_ref, wb_ref, o_ref):
    d = o_ref.shape[1]
    acc = None
    for i, r in enumerate((oa_ref, ob_ref, oc_ref)):
        t = jnp.dot(r[...], wb_ref[i], preferred_element_type=F32)
        term = sg_ref[:, i * d:(i + 1) * d].astype(F32) * t
        acc = term if acc is None else acc + term
    o_ref[...] = acc.astype(o_ref.dtype)


def _merge(oa, ob, oc, sg, wb, tm):
    rows, n = oa.shape
    d = wb.shape[2]
    row = lambda i: (i, 0)
    vmem = int(wb.size) * 2 + 2 * tm * (3 * n * 2 + N_BRANCH * d * 2 + d * 2) + 3 * tm * d * 4 + (8 << 20)
    return pl.pallas_call(
        _merge_kernel,
        grid=(rows // tm,),
        in_specs=[pl.BlockSpec((tm, n), row)] * 3
                 + [pl.BlockSpec((tm, N_BRANCH * d), row), _const_spec(wb.shape)],
        out_specs=pl.BlockSpec((tm, d), row),
        out_shape=jax.ShapeDtypeStruct((rows, d), BF16),
        compiler_params=_params(("parallel",), vmem),
        name="merge",
    )(oa, ob, oc, sg, wb)


def _out_kernel(m_ref, x_ref, w_ref, g_ref, o_ref):
    y = x_ref[...] + jnp.dot(m_ref[...], w_ref[...], preferred_element_type=F32)
    o_ref[...] = (_rms(y) * g_ref[...]).astype(o_ref.dtype)


def _outproj(merged, x, w, g, tm):
    rows, d = x.shape
    row = lambda i: (i, 0)
    vmem = d * d * 2 + 2 * tm * d * (2 + 4 + 4) + 3 * tm * d * 4 + (8 << 20)
    return pl.pallas_call(
        _out_kernel,
        grid=(rows // tm,),
        in_specs=[pl.BlockSpec((tm, d), row), pl.BlockSpec((tm, d), row),
                  _const_spec(w.shape), _const_spec(g.shape)],
        out_specs=pl.BlockSpec((tm, d), row),
        out_shape=jax.ShapeDtypeStruct((rows, d), x.dtype),
        compiler_params=_params(("parallel",), vmem),
        name="out_proj",
    )(merged, x, w, g)


def _wprep_kernel(wt_ref, ws_ref, wz_ref, wg_ref):
    n_small = Q_LORA + KV_LORA + MLA_ROPE
    z0 = n_small + (_C_END - _C_QG)
    cols = wt_ref.shape[1]
    ws_ref[:, :_C_QG] = jnp.concatenate(
        [wt_ref[:n_small, :], jnp.zeros((_C_QG - n_small, cols), F32)], axis=0).T.astype(BF16)
    ws_ref[:, _C_QG:] = wt_ref[n_small:z0, :].T.astype(BF16)
    wz_ref[...] = wt_ref[z0:z0 + N_BRANCH * BRANCH_W, :].T.astype(BF16)
    wg_ref[...] = wt_ref[z0 + N_BRANCH * BRANCH_W:, :].T.astype(BF16)


def _wprep(wt, tc):
    n, d = wt.shape
    widths = (_C_END, N_BRANCH * BRANCH_W, N_BRANCH * D_MODEL)
    return pl.pallas_call(
        _wprep_kernel,
        grid=(d // tc,),
        in_specs=[pl.BlockSpec((n, tc), lambda i: (0, i))],
        out_specs=[pl.BlockSpec((tc, w), lambda i: (i, 0)) for w in widths],
        out_shape=[jax.ShapeDtypeStruct((d, w), BF16) for w in widths],
        compiler_params=_params(("parallel",), 2 * tc * n * 4 + 2 * tc * sum(widths) * 2 + tc * n * 8 + (8 << 20)),
        name="weight_prep",
    )(wt)


def _rope_tables(seq):
    rows = seq // GRID_W
    row = np.repeat(np.arange(rows, dtype=np.float32), GRID_W)
    col = np.tile(np.arange(GRID_W, dtype=np.float32), rows)

    def table(width):
        half = width // 4
        freqs = (ROPE_BASE ** (-np.arange(half, dtype=np.float32) / half)).astype(np.float32)
        parts_c, parts_s = [], []
        for pos in (row, col):
            ang = (pos[:, None] * freqs[None, :]).astype(np.float64)
            c, sn = np.cos(ang), np.sin(ang)
            parts_c += [c, c]
            parts_s += [-sn, sn]
        pad = ((0, 0), (0, LANES - width))
        c = np.pad(np.concatenate(parts_c, axis=-1), pad)
        sn = np.pad(np.concatenate(parts_s, axis=-1), pad)
        return jnp.asarray(c, F32), jnp.asarray(sn, F32)

    cm, sm = table(MLA_ROPE)
    cg, sg = table(GQA_HEAD_DIM)
    return cm, sm, cg, sg


def _prep_layer(w_in, w_q_b, w_kv_b, w_mem_kv, w_branch, w_out):
    w_small, w_z, w_gl = _wprep(w_in.T, 2 * LANES)
    wvgt = w_small[:, _C_VG:_C_QM].T
    wqb = w_q_b.reshape(Q_LORA, MLA_HEADS, MLA_NOPE + MLA_ROPE)
    wqb = jnp.pad(wqb, ((0, 0), (0, 0), (0, MLA_QK_PAD - MLA_NOPE - MLA_ROPE)))
    wqb = wqb.reshape(Q_LORA, MLA_HEADS * MLA_QK_PAD).astype(BF16)
    wkv = w_kv_b.reshape(KV_LORA, MLA_HEADS, MLA_NOPE + MLA_V)
    wkb = wkv[:, :, :MLA_NOPE].reshape(KV_LORA, -1).astype(BF16)
    wvbt = wkv[:, :, MLA_NOPE:].reshape(KV_LORA, -1).T.astype(BF16)
    return (w_small, wvgt, w_z, w_gl, wqb, wkb, wvbt, w_mem_kv.astype(BF16), w_branch.astype(BF16),
            w_out.astype(BF16))


def _tile(n, pref):
    t = min(n, pref)
    assert n % t == 0, (n, t)
    return t


def _layer(x, mem, tabs, batch, seq, norm_g, mla_q_norm_g, mla_kv_norm_g, gqa_q_norm_g, gqa_k_norm_g,
           mem_norm_g, final_g, prepped):
    w_small, wvgt, w_z, w_gl, wqb, wkb, wvbt, wmem, wbr, wout = prepped
    rows = batch * seq
    r2 = lambda g: g.reshape(1, -1)
    tm = _tile(seq, 512)

    h, qmla, kmla, vmla, qg, kg, vg, qm = _small(
        x, r2(norm_g), w_small, wvgt, r2(mla_q_norm_g), wqb, r2(mla_kv_norm_g), wkb, wvbt, r2(gqa_q_norm_g),
        r2(gqa_k_norm_g), tabs, seq, tm)
    tg = _tile(rows, 1024)
    sz = _gate(h, w_z, True, tg, GATE_TN, "gate_z")
    sg = _gate(h, w_gl, False, tg, GATE_TN, "gate_g")

    bq = _tile(seq // 2, 512)
    o_mla = _attn(qmla, kmla, vmla, sz, 0, batch, seq, MLA_HEADS, MLA_HEADS, MLA_QK_PAD, bq, "attn_mla")
    o_gqa = _attn(qg, kg, vg, sz, BRANCH_W // LANES, batch, seq, GQA_HEADS, GQA_KV_HEADS, GQA_HEAD_DIM, bq,
                  "attn_gqa")
    km, vm = _memkv(mem, r2(mem_norm_g), wmem)
    o_mem = _memattn(qm, km, vm, sz, batch, seq, _tile(seq, 1024))

    merged = _merge(o_mla, o_gqa, o_mem, sg, wbr, tm)
    return _outproj(merged, x, wout, r2(final_g), tm)


def _trunk(x, mem, tabs, gains, prepped):
    batch, seq, d = x.shape
    y = _layer(x.reshape(batch * seq, d), mem.reshape(batch * mem.shape[1], d), tabs, batch, seq, *gains,
               prepped)
    return y.reshape(batch, seq, d)


def kernel(x_prompt, x_sample, mem_prompt, mem_sample, norm_g, w_in, mla_q_norm_g, w_q_b, mla_kv_norm_g,
           w_kv_b, gqa_q_norm_g, gqa_k_norm_g, mem_norm_g, w_mem_kv, w_branch, w_out, final_norm_g):
    assert w_in.shape[0] == 1
    first = lambda a: a.reshape(a.shape[1:])
    prepped = _prep_layer(first(w_in), first(w_q_b), first(w_kv_b), first(w_mem_kv), first(w_branch),
                          first(w_out))
    gains = (first(norm_g), first(mla_q_norm_g), first(mla_kv_norm_g), first(gqa_q_norm_g),
             first(gqa_k_norm_g), first(mem_norm_g), final_norm_g)
    assert x_prompt.shape[1] == x_sample.shape[1]
    tabs = _rope_tables(x_prompt.shape[1])
    return (_trunk(x_prompt, mem_prompt, tabs, gains, prepped),
            _trunk(x_sample, mem_sample, tabs, gains, prepped))
```
